```python
import math
import jax, jax.numpy as jnp
from jax import lax
import numpy as np

D_MODEL = 1024
BATCH = 8
SEQ = 2048
DEPTH = 4
DEC_BATCH = 128
DEC_SEQ = 4
PAST_LEN = 16384
PAGE_SIZE = 128

MIX_WIDTH = 2 * D_MODEL
CONV_WIDTH = 3 * MIX_WIDTH // 8
LRU_WIDTH = 3 * MIX_WIDTH // 8
MEM_WIDTH = MIX_WIDTH - CONV_WIDTH - LRU_WIDTH
LRU_HEADS = 8
LRU_HEAD_DIM = LRU_WIDTH // LRU_HEADS
MEM_HEADS = 4
MEM_HEAD_DIM = MEM_WIDTH // MEM_HEADS
N_MEM = 256
CONV_K = 31
LRU_CONV_K = 4
LRU_C = 8.0
EPS = 1e-6
IN_WIDTH = 3 * CONV_WIDTH + 2 * LRU_WIDTH + 2 * MEM_WIDTH

kernel_name = 'hybrid_conformer_rglru_memxattn_step'


def rmsnorm(x, g):
    xf = x.astype(jnp.float32)
    y = xf * lax.rsqrt(jnp.mean(xf * xf, axis=-1, keepdims=True) + EPS)
    return (y * g.astype(jnp.float32)).astype(x.dtype)


def layernorm(x, g, b):
    xf = x.astype(jnp.float32)
    mu = jnp.mean(xf, axis=-1, keepdims=True)
    var = jnp.mean(jnp.square(xf - mu), axis=-1, keepdims=True)
    y = (xf - mu) * lax.rsqrt(var + EPS)
    return (y * g.astype(jnp.float32) + b.astype(jnp.float32)).astype(x.dtype)


def depthwise_causal(xp, w, b):
    c = xp.shape[-1]
    y = lax.conv_general_dilated(xp, w[:, None, :].astype(xp.dtype), window_strides=(1,),
                                 padding='VALID', dimension_numbers=('NWC', 'WIO', 'NWC'),
                                 feature_group_count=c)
    return y + b


def linear_recurrence(a, bx, h0):
    bx = bx.at[:, 0].add(a[:, 0] * h0)
    def combine(l, r):
        return (l[0] * r[0], r[0] * l[1] + r[1])
    _, h = lax.associative_scan(combine, (a, bx), axis=1)
    return h


def memory_kv(mem, g, wk, wv):
    mn = rmsnorm(mem, g)
    b = mem.shape[0]
    k = (mn @ wk).reshape(b, N_MEM, MEM_HEADS, MEM_HEAD_DIM)
    v = (mn @ wv).reshape(b, N_MEM, MEM_HEADS, MEM_HEAD_DIM)
    return k, v


def mixer_layer(x, mem_k, mem_v, conv_buf, lru_buf, h0,
                g_pre, w_in, conv_w, conv_b, ln_g, ln_b,
                lru_conv_w, lru_conv_b, wa, ba, wx, bx, lam, w_out, g_post):
    b, t, _ = x.shape
    xn = rmsnorm(x, g_pre)
    proj = xn @ w_in
    s = np.cumsum([CONV_WIDTH, CONV_WIDTH, CONV_WIDTH, LRU_WIDTH, LRU_WIDTH, MEM_WIDTH])
    a_c, b_c, g_c, x_r, g_r, q, g_q = jnp.split(proj, [int(v) for v in s], axis=-1)

    u = a_c * jax.nn.sigmoid(b_c)
    up = jnp.concatenate([conv_buf.astype(u.dtype), u], axis=1)
    new_conv_buf = up[:, up.shape[1] - (CONV_K - 1):]
    c = layernorm(depthwise_causal(up, conv_w, conv_b), ln_g, ln_b)
    c = jax.nn.silu(c) * jax.nn.silu(g_c)

    xp = jnp.concatenate([lru_buf.astype(x_r.dtype), x_r], axis=1)
    new_lru_buf = xp[:, xp.shape[1] - (LRU_CONV_K - 1):]
    xc = depthwise_causal(xp, lru_conv_w, lru_conv_b)
    xh = xc.reshape(b, t, LRU_HEADS, LRU_HEAD_DIM)
    r = jax.nn.sigmoid(jnp.einsum('bthi,hij->bthj', xh, wa).reshape(b, t, LRU_WIDTH) + ba)
    ig = jax.nn.sigmoid(jnp.einsum('bthi,hij->bthj', xh, wx).reshape(b, t, LRU_WIDTH) + bx)
    log_a = -LRU_C * r.astype(jnp.float32) * jax.nn.softplus(-lam.astype(jnp.float32))
    a = jnp.exp(log_a)
    mult = jnp.sqrt(-jnp.expm1(2.0 * log_a))
    bx_t = mult * (ig * xc).astype(jnp.float32)
    h = linear_recurrence(a, bx_t, h0.astype(jnp.float32))
    new_h = h[:, -1]
    rr = h.astype(x.dtype) * jax.nn.silu(g_r)

    qh = q.reshape(b, t, MEM_HEADS, MEM_HEAD_DIM)
    sc = jnp.einsum('bthd,bmhd->bhtm', qh, mem_k).astype(jnp.float32) / math.sqrt(MEM_HEAD_DIM)
    p = jax.nn.softmax(sc, axis=-1).astype(x.dtype)
    o = jnp.einsum('bhtm,bmhd->bthd', p, mem_v).reshape(b, t, MEM_WIDTH) * jax.nn.silu(g_q)

    out = jnp.concatenate([c, rr, o], axis=-1) @ w_out
    y = x + rmsnorm(out, g_post)
    return y, new_conv_buf, new_lru_buf, new_h.astype(h0.dtype)


def setup_inputs(seed: int = 0) -> dict:
    key = jax.random.key(seed)
    ks = jax.random.split(key, 32)
    f = jnp.float32

    def nrm(k, shape, s):
        return jax.random.normal(k, shape, f) * s

    a0 = jax.random.uniform(ks[20], (DEPTH, LRU_WIDTH), f, 0.9, 0.999)
    return {
        'x_prompt': nrm(ks[0], (BATCH, SEQ, D_MODEL), 1.0),
        'x_sample': nrm(ks[1], (DEC_BATCH, DEC_SEQ, D_MODEL), 1.0),
        'mem_prompt': nrm(ks[2], (BATCH, N_MEM, D_MODEL), 1.0),
        'cache_conv': nrm(ks[3], (DEPTH, DEC_BATCH, CONV_K - 1, CONV_WIDTH), 0.5),
        'cache_lru_conv': nrm(ks[4], (DEPTH, DEC_BATCH, LRU_CONV_K - 1, LRU_WIDTH), 0.5),
        'state_lru_h': nrm(ks[5], (DEPTH, DEC_BATCH, LRU_WIDTH), 0.5),
        'cache_mem_k': nrm(ks[6], (DEPTH, DEC_BATCH, N_MEM, MEM_HEADS, MEM_HEAD_DIM), 1.0),
        'cache_mem_v': nrm(ks[7], (DEPTH, DEC_BATCH, N_MEM, MEM_HEADS, MEM_HEAD_DIM), 1.0),
        'norm_pre_g': 1.0 + nrm(ks[8], (DEPTH, D_MODEL), 0.05),
        'w_in': nrm(ks[9], (DEPTH, D_MODEL, IN_WIDTH), D_MODEL ** -0.5),
        'conv_w': nrm(ks[10], (DEPTH, CONV_K, CONV_WIDTH), CONV_K ** -0.5),
        'conv_b': nrm(ks[11], (DEPTH, CONV_WIDTH), 0.02),
        'conv_ln_g': 1.0 + nrm(ks[12], (DEPTH, CONV_WIDTH), 0.05),
        'conv_ln_b': nrm(ks[13], (DEPTH, CONV_WIDTH), 0.02),
        'lru_conv_w': nrm(ks[14], (DEPTH, LRU_CONV_K, LRU_WIDTH), LRU_CONV_K ** -0.5),
        'lru_conv_b': nrm(ks[15], (DEPTH, LRU_WIDTH), 0.02),
        'lru_wa': nrm(ks[16], (DEPTH, LRU_HEADS, LRU_HEAD_DIM, LRU_HEAD_DIM), LRU_HEAD_DIM ** -0.5),
        'lru_ba': nrm(ks[17], (DEPTH, LRU_WIDTH), 0.02),
        'lru_wx': nrm(ks[18], (DEPTH, LRU_HEADS, LRU_HEAD_DIM, LRU_HEAD_DIM), LRU_HEAD_DIM ** -0.5),
        'lru_bx': nrm(ks[19], (DEPTH, LRU_WIDTH), 0.02),
        'lru_lambda': jnp.log(a0) - jnp.log1p(-a0),
        'mem_norm_g': 1.0 + nrm(ks[21], (DEPTH, D_MODEL), 0.05),
        'w_mem_k': nrm(ks[22], (DEPTH, D_MODEL, MEM_WIDTH), D_MODEL ** -0.5),
        'w_mem_v': nrm(ks[23], (DEPTH, D_MODEL, MEM_WIDTH), D_MODEL ** -0.5),
        'w_out': nrm(ks[24], (DEPTH, MIX_WIDTH, D_MODEL), MIX_WIDTH ** -0.5),
        'norm_post_g': 1.0 + nrm(ks[25], (DEPTH, D_MODEL), 0.05),
    }


def reference(x_prompt, x_sample, mem_prompt, cache_conv, cache_lru_conv, state_lru_h,
              cache_mem_k, cache_mem_v, norm_pre_g, w_in, conv_w, conv_b, conv_ln_g, conv_ln_b,
              lru_conv_w, lru_conv_b, lru_wa, lru_ba, lru_wx, lru_bx, lru_lambda,
              mem_norm_g, w_mem_k, w_mem_v, w_out, norm_post_g):
    bp = x_prompt.shape[0]
    dt = x_prompt.dtype
    xp, xs = x_prompt, x_sample
    p_conv, p_lconv, p_h, p_mk, p_mv = [], [], [], [], []
    s_conv, s_lconv, s_h = [], [], []
    for l in range(DEPTH):
        w = (norm_pre_g[l], w_in[l], conv_w[l], conv_b[l], conv_ln_g[l], conv_ln_b[l],
             lru_conv_w[l], lru_conv_b[l], lru_wa[l], lru_ba[l], lru_wx[l], lru_bx[l],
             lru_lambda[l], w_out[l], norm_post_g[l])
        mk, mv = memory_kv(mem_prompt, mem_norm_g[l], w_mem_k[l], w_mem_v[l])
        xp, cb, lb, hh = mixer_layer(
            xp, mk, mv,
            jnp.zeros((bp, CONV_K - 1, CONV_WIDTH), dt),
            jnp.zeros((bp, LRU_CONV_K - 1, LRU_WIDTH), dt),
            jnp.zeros((bp, LRU_WIDTH), dt), *w)
        p_conv.append(cb); p_lconv.append(lb); p_h.append(hh); p_mk.append(mk); p_mv.append(mv)
        xs, cb2, lb2, hh2 = mixer_layer(
            xs, cache_mem_k[l], cache_mem_v[l], cache_conv[l], cache_lru_conv[l],
            state_lru_h[l], *w)
        s_conv.append(cb2); s_lconv.append(lb2); s_h.append(hh2)
    return (xp, xs,
            jnp.stack(p_conv), jnp.stack(p_lconv), jnp.stack(p_h), jnp.stack(p_mk), jnp.stack(p_mv),
            jnp.stack(s_conv), jnp.stack(s_lconv), jnp.stack(s_h))
```

```python
import math

import jax
import jax.numpy as jnp
from jax import lax
from jax.experimental import pallas as pl
from jax.experimental.pallas import tpu as pltpu

D_MODEL = 1024
MIX_WIDTH = 2048
CONV_WIDTH = 768
LRU_WIDTH = 768
MEM_WIDTH = 512
MEM_HEADS = 4
MEM_HEAD_DIM = 128
N_MEM = 256
CONV_K = 31
LRU_CONV_K = 4
LRU_C = 8.0
EPS = 1e-6
IN_WIDTH = 4864

OFF_A, OFF_B, OFF_GC = 0, 768, 1536
OFF_XR, OFF_GR = 2304, 3072
OFF_Q, OFF_GQ = 3840, 4352

LANES = 128
SUBLANES = 8
NCHUNK = CONV_WIDTH // LANES
CONV_HIST_PAD = 32
LRU_HIST_PAD = 8
PROMPT_TM = 256
CONV_ROWS = 32
SCAN_ROWS = 16
SAMPLE_RB = 32
SAMPLE_AB = 8
VMEM_LIMIT = 56 * 1024 * 1024

BF16 = jnp.bfloat16
F32 = jnp.float32


def _sigmoid(x):
    return 1.0 / (1.0 + jnp.exp(-x))


def _silu(x):
    return x * _sigmoid(x)


def _rmsnorm(x, g):
    return x * lax.rsqrt(jnp.mean(x * x, axis=-1, keepdims=True) + EPS) * g


def _layernorm(x, g, b):
    mu = jnp.mean(x, axis=-1, keepdims=True)
    d = x - mu
    var = jnp.mean(d * d, axis=-1, keepdims=True)
    return d * lax.rsqrt(var + EPS) * g + b


def _softplus(z):
    return jnp.maximum(z, 0.0) + jnp.log1p(jnp.exp(-jnp.abs(z)))


def _lru_coeffs(gate_a, gate_x, xc, ba, bx, sp):
    r = _sigmoid(gate_a + ba)
    ig = _sigmoid(gate_x + bx)
    log_a = (-LRU_C) * r * sp
    a = jnp.exp(log_a)
    mult = jnp.sqrt(jnp.tanh(-log_a) * (1.0 + a * a))
    return a, mult * (ig * xc)


def _group_scan(a, b, period):
    row = lax.broadcasted_iota(jnp.int32, a.shape, 0) % period
    s = 1
    while s < period:
        keep = row >= s
        a_s = pltpu.roll(a, s, 0)
        b_s = pltpu.roll(b, s, 0)
        b = jnp.where(keep, a * b_s + b, b)
        a = jnp.where(keep, a * a_s, a)
        s *= 2
    return a, b


def _attend(q_bf, k_bf, v_bf):
    s = lax.dot_general(q_bf, k_bf, (((1,), (1,)), ((), ())), preferred_element_type=F32)
    s = s * (1.0 / math.sqrt(MEM_HEAD_DIM))
    e = jnp.exp(s - jnp.max(s, axis=-1, keepdims=True))
    l = jnp.sum(e, axis=-1, keepdims=True)
    o = jnp.dot(e.astype(BF16), v_bf, preferred_element_type=F32)
    return o / l


def _store_slabs(dst_ref, row0, val):
    for c in range(NCHUNK):
        dst_ref[c, row0:row0 + val.shape[0], :] = val[:, c * LANES:(c + 1) * LANES]


def _load_slabs(src_ref, row0, nrows):
    return jnp.concatenate([src_ref[c, row0:row0 + nrows, :] for c in range(NCHUNK)], axis=1)


def _store_taps(dst_ref, w):
    for k in range(w.shape[0]):
        for c in range(NCHUNK):
            dst_ref[c, k * SUBLANES:(k + 1) * SUBLANES, :] = jnp.broadcast_to(
                w[k:k + 1, c * LANES:(c + 1) * LANES], (SUBLANES, LANES))


def _window_conv(buf_ref, w_ref, start, taps, ngroups):
    cols = []
    for c in range(NCHUNK):
        accs = [jnp.zeros((SUBLANES, LANES), F32) for _ in range(ngroups)]
        for k in range(taps):
            w8 = w_ref[c, k * SUBLANES:(k + 1) * SUBLANES, :]
            for j in range(ngroups):
                accs[j] = accs[j] + buf_ref[c, pl.ds(start + k + j * SUBLANES, SUBLANES), :] * w8
        cols.append(accs[0] if ngroups == 1 else jnp.concatenate(accs, axis=0))
    return jnp.concatenate(cols, axis=1)


def _mem_kv_kernel(mem_ref, g_ref, wk_ref, wv_ref, k_ref, v_ref):
    mn = _rmsnorm(mem_ref[...], g_ref[...]).astype(BF16)
    k_ref[...] = jnp.dot(mn, wk_ref[...], preferred_element_type=F32)
    v_ref[...] = jnp.dot(mn, wv_ref[...], preferred_element_type=F32)


def _mem_kv(mem_prompt, mem_norm_g, wk_bf, wv_bf):
    depth, batch = wk_bf.shape[0], mem_prompt.shape[0]
    out = jax.ShapeDtypeStruct((depth, batch, N_MEM, MEM_WIDTH), F32)
    return pl.pallas_call(
        _mem_kv_kernel,
        grid=(depth, batch),
        in_specs=[
            pl.BlockSpec((None, N_MEM, D_MODEL), lambda l, b: (b, 0, 0)),
            pl.BlockSpec((None, 1, D_MODEL), lambda l, b: (l, 0, 0)),
            pl.BlockSpec((None, D_MODEL, MEM_WIDTH), lambda l, b: (l, 0, 0)),
            pl.BlockSpec((None, D_MODEL, MEM_WIDTH), lambda l, b: (l, 0, 0)),
        ],
        out_specs=[
            pl.BlockSpec((None, None, N_MEM, MEM_WIDTH), lambda l, b: (l, b, 0, 0)),
            pl.BlockSpec((None, None, N_MEM, MEM_WIDTH), lambda l, b: (l, b, 0, 0)),
        ],
        out_shape=[out, out],
        compiler_params=pltpu.CompilerParams(dimension_semantics=("arbitrary", "arbitrary")),
        name="mem_kv",
    )(mem_prompt, mem_norm_g.reshape(depth, 1, D_MODEL), wk_bf, wv_bf)


def _prompt_layer_kernel(x_ref, k_ref, v_ref, gpre_ref, win_ref, cw_ref, cb_ref, lng_ref, lnb_ref,
                         lcw_ref, lcb_ref, wg_ref, ba_ref, bx_ref, lam_ref, wout_ref, gpost_ref,
                         y_ref, nconv_ref, nlru_ref, nh_ref,
                         proj_ref, ubuf_ref, xrbuf_ref, cws_ref, lcws_ref, xc_ref, mix_ref, hcarry_ref):
    b = pl.program_id(0)
    t = pl.program_id(1)
    tm = x_ref.shape[0]

    @pl.when(t == 0)
    def _():
        for c in range(NCHUNK):
            ubuf_ref[c, 0:CONV_HIST_PAD, :] = jnp.zeros((CONV_HIST_PAD, LANES), F32)
            xrbuf_ref[c, 0:LRU_HIST_PAD, :] = jnp.zeros((LRU_HIST_PAD, LANES), F32)
        hcarry_ref[...] = jnp.zeros(hcarry_ref.shape, F32)
        _store_taps(cws_ref, cw_ref[...])
        _store_taps(lcws_ref, lcw_ref[...])

    xn = _rmsnorm(x_ref[...], gpre_ref[...]).astype(BF16)
    proj_ref[...] = jnp.dot(xn, win_ref[...], preferred_element_type=F32)

    _store_slabs(ubuf_ref, CONV_HIST_PAD,
                 proj_ref[:, OFF_A:OFF_A + CONV_WIDTH] * _sigmoid(proj_ref[:, OFF_B:OFF_B + CONV_WIDTH]))
    _store_slabs(xrbuf_ref, LRU_HIST_PAD, proj_ref[:, OFF_XR:OFF_XR + LRU_WIDTH])

    def conv_chunk(i, carry):
        r0 = pl.multiple_of(i * CONV_ROWS, CONV_ROWS)
        acc = _window_conv(ubuf_ref, cws_ref, r0 + (CONV_HIST_PAD - (CONV_K - 1)), CONV_K,
                           CONV_ROWS // SUBLANES)
        c = _layernorm(acc + cb_ref[...], lng_ref[...], lnb_ref[...])
        c = _silu(c) * _silu(proj_ref[pl.ds(r0, CONV_ROWS), OFF_GC:OFF_GC + CONV_WIDTH])
        mix_ref[pl.ds(r0, CONV_ROWS), 0:CONV_WIDTH] = c.astype(BF16)

        xc = _window_conv(xrbuf_ref, lcws_ref, r0 + (LRU_HIST_PAD - (LRU_CONV_K - 1)), LRU_CONV_K,
                          CONV_ROWS // SUBLANES)
        xc_ref[pl.ds(r0, CONV_ROWS), :] = xc + lcb_ref[...]
        return carry

    lax.fori_loop(0, tm // CONV_ROWS, conv_chunk, 0)

    for c in range(NCHUNK):
        ubuf_ref[c, 0:CONV_HIST_PAD, :] = ubuf_ref[c, tm:tm + CONV_HIST_PAD, :]
        xrbuf_ref[c, 0:LRU_HIST_PAD, :] = xrbuf_ref[c, tm:tm + LRU_HIST_PAD, :]

    proj_ref[:, 0:2 * LRU_WIDTH] = jnp.dot(xc_ref[...].astype(BF16), wg_ref[...], preferred_element_type=F32)

    sp = _softplus(-lam_ref[...])
    ba = ba_ref[...]
    bx = bx_ref[...]

    def scan_chunk(i, hprev):
        r0 = pl.multiple_of(i * SCAN_ROWS, SCAN_ROWS)
        hs = []
        for j in range(SCAN_ROWS // SUBLANES):
            rows = pl.ds(r0 + j * SUBLANES, SUBLANES)
            a, bt = _lru_coeffs(proj_ref[rows, 0:LRU_WIDTH], proj_ref[rows, LRU_WIDTH:2 * LRU_WIDTH],
                                xc_ref[rows, :], ba, bx, sp)
            a, bt = _group_scan(a, bt, SUBLANES)
            h = a * hprev + bt
            hprev = jnp.broadcast_to(h[SUBLANES - 1:SUBLANES, :], (SUBLANES, LRU_WIDTH))
            hs.append(h)
        h16 = jnp.concatenate(hs, axis=0)
        rr = h16 * _silu(proj_ref[pl.ds(r0, SCAN_ROWS), OFF_GR:OFF_GR + LRU_WIDTH])
        mix_ref[pl.ds(r0, SCAN_ROWS), CONV_WIDTH:CONV_WIDTH + LRU_WIDTH] = rr.astype(BF16)
        return hprev

    hlast = lax.fori_loop(0, tm // SCAN_ROWS, scan_chunk, hcarry_ref[...])
    hcarry_ref[...] = hlast

    for h in range(MEM_HEADS):
        cols = slice(h * MEM_HEAD_DIM, (h + 1) * MEM_HEAD_DIM)
        q = proj_ref[:, OFF_Q + h * MEM_HEAD_DIM:OFF_Q + (h + 1) * MEM_HEAD_DIM].astype(BF16)
        o = _attend(q, k_ref[:, cols].astype(BF16), v_ref[:, cols].astype(BF16))
        o = o * _silu(proj_ref[:, OFF_GQ + h * MEM_HEAD_DIM:OFF_GQ + (h + 1) * MEM_HEAD_DIM])
        mix_ref[:, CONV_WIDTH + LRU_WIDTH + h * MEM_HEAD_DIM:
                CONV_WIDTH + LRU_WIDTH + (h + 1) * MEM_HEAD_DIM] = o.astype(BF16)

    out = jnp.dot(mix_ref[...], wout_ref[...], preferred_element_type=F32)
    y_ref[...] = x_ref[...] + _rmsnorm(out, gpost_ref[...])

    @pl.when(t == pl.num_programs(1) - 1)
    def _():
        nconv_ref[...] = _load_slabs(ubuf_ref, CONV_HIST_PAD - (CONV_K - 1), CONV_K - 1)
        nlru_ref[...] = _load_slabs(xrbuf_ref, LRU_HIST_PAD - (LRU_CONV_K - 1), LRU_CONV_K - 1)
        nh_ref[pl.ds(b, 1), :] = hlast[0:1, :]


def _const_spec(shape, l):
    nd = len(shape)
    return pl.BlockSpec((None,) + tuple(shape[1:]), lambda *_: (l,) + (0,) * (nd - 1))


def _prompt_layer(l, x, mk, mv, p):
    batch, seq, _ = x.shape
    tm = PROMPT_TM
    grid = (batch, seq // tm)
    row_params = [p["norm_pre_g"], p["w_in"], p["conv_w"], p["conv_b"], p["conv_ln_g"], p["conv_ln_b"],
                  p["lru_conv_w"], p["lru_conv_b"], p["w_gate"], p["lru_ba"], p["lru_bx"], p["lru_lambda"],
                  p["w_out"], p["norm_post_g"]]
    in_specs = [
        pl.BlockSpec((None, tm, D_MODEL), lambda b, t: (b, t, 0)),
        pl.BlockSpec((None, None, N_MEM, MEM_WIDTH), lambda b, t: (l, b, 0, 0)),
        pl.BlockSpec((None, None, N_MEM, MEM_WIDTH), lambda b, t: (l, b, 0, 0)),
    ] + [_const_spec(a.shape, l) for a in row_params]
    out_shape = [
        jax.ShapeDtypeStruct((batch, seq, D_MODEL), F32),
        jax.ShapeDtypeStruct((batch, CONV_K - 1, CONV_WIDTH), F32),
        jax.ShapeDtypeStruct((batch, LRU_CONV_K - 1, LRU_WIDTH), F32),
        jax.ShapeDtypeStruct((batch, LRU_WIDTH), F32),
    ]
    out_specs = [
        pl.BlockSpec((None, tm, D_MODEL), lambda b, t: (b, t, 0)),
        pl.BlockSpec((None, CONV_K - 1, CONV_WIDTH), lambda b, t: (b, 0, 0)),
        pl.BlockSpec((None, LRU_CONV_K - 1, LRU_WIDTH), lambda b, t: (b, 0, 0)),
        pl.BlockSpec((batch, LRU_WIDTH), lambda b, t: (0, 0)),
    ]
    scratch = [
        pltpu.VMEM((tm, IN_WIDTH), F32),
        pltpu.VMEM((NCHUNK, CONV_HIST_PAD + tm, LANES), F32),
        pltpu.VMEM((NCHUNK, LRU_HIST_PAD + tm, LANES), F32),
        pltpu.VMEM((NCHUNK, CONV_K * SUBLANES, LANES), F32),
        pltpu.VMEM((NCHUNK, LRU_CONV_K * SUBLANES, LANES), F32),
        pltpu.VMEM((tm, LRU_WIDTH), F32),
        pltpu.VMEM((tm, MIX_WIDTH), BF16),
        pltpu.VMEM((SUBLANES, LRU_WIDTH), F32),
    ]
    return pl.pallas_call(
        _prompt_layer_kernel,
        grid=grid,
        in_specs=in_specs,
        out_specs=out_specs,
        out_shape=out_shape,
        scratch_shapes=scratch,
        compiler_params=pltpu.CompilerParams(dimension_semantics=("arbitrary", "arbitrary"),
                                             vmem_limit_bytes=VMEM_LIMIT),
        name=f"prompt_layer{l}",
    )(x, mk, mv, *row_params)


def _sample_in_kernel(x_ref, cconv_ref, clru_ref, h0_ref, gpre_ref, win_ref, cw_ref, cb_ref, lng_ref, lnb_ref,
                      lcw_ref, lcb_ref, wg_ref, ba_ref, bx_ref, lam_ref,
                      mixcr_ref, qg_ref, nconv_ref, nlru_ref, nh_ref,
                      proj_ref, u_ref, xr_ref, cwin_ref, lwin_ref, cws_ref, lcws_ref,
                      conv_ref, xc_ref, h0rep_ref, h_ref):
    rows = x_ref.shape[0]
    nb = cconv_ref.shape[0]
    t_new = rows // nb
    per_group = SUBLANES // t_new
    hist_c = CONV_K - 1
    hist_l = LRU_CONV_K - 1
    row8 = lax.broadcasted_iota(jnp.int32, (SUBLANES, CONV_WIDTH), 0)

    @pl.when(pl.program_id(0) == 0)
    def _():
        _store_taps(cws_ref, cw_ref[...])
        _store_taps(lcws_ref, lcw_ref[...])

    xn = _rmsnorm(x_ref[...], gpre_ref[...]).astype(BF16)
    proj_ref[...] = jnp.dot(xn, win_ref[...], preferred_element_type=F32)
    u_ref[...] = proj_ref[:, OFF_A:OFF_A + CONV_WIDTH] * _sigmoid(proj_ref[:, OFF_B:OFF_B + CONV_WIDTH])
    xr_ref[...] = proj_ref[:, OFF_XR:OFF_XR + LRU_WIDTH]
    qg_ref[...] = proj_ref[:, OFF_Q:OFF_Q + 2 * MEM_WIDTH]

    c_new0 = CONV_HIST_PAD
    l_new0 = LRU_HIST_PAD

    def per_group_body(g, carry):
        r0 = pl.multiple_of(g * SUBLANES, SUBLANES)
        _store_slabs(cwin_ref, c_new0, u_ref[pl.ds(r0, SUBLANES), :])
        _store_slabs(lwin_ref, l_new0, xr_ref[pl.ds(r0, SUBLANES), :])
        conv8 = jnp.zeros((SUBLANES, CONV_WIDTH), F32)
        xc8 = jnp.zeros((SUBLANES, LRU_WIDTH), F32)
        h08 = jnp.zeros((SUBLANES, LRU_WIDTH), F32)
        for j in range(per_group):
            bi = g * per_group + j
            mine = (row8 // t_new) == j
            c0 = c_new0 + j * t_new - hist_c
            _store_slabs(cwin_ref, c0, cconv_ref[bi])
            nconv_ref[bi] = _load_slabs(cwin_ref, c0 + t_new, hist_c)
            acc = _window_conv(cwin_ref, cws_ref, c0, CONV_K, 1)
            conv8 = jnp.where(mine, pltpu.roll(acc, j * t_new, 0) if j else acc, conv8)
            l0 = l_new0 + j * t_new - hist_l
            _store_slabs(lwin_ref, l0, clru_ref[bi])
            nlru_ref[bi] = _load_slabs(lwin_ref, l0 + t_new, hist_l)
            xc = _window_conv(lwin_ref, lcws_ref, l0, LRU_CONV_K, 1)
            xc8 = jnp.where(mine, pltpu.roll(xc, j * t_new, 0) if j else xc, xc8)
            h08 = jnp.where(mine, jnp.broadcast_to(h0_ref[bi], (SUBLANES, LRU_WIDTH)), h08)
        conv_ref[pl.ds(r0, SUBLANES), :] = conv8 + cb_ref[...]
        xc_ref[pl.ds(r0, SUBLANES), :] = xc8 + lcb_ref[...]
        h0rep_ref[pl.ds(r0, SUBLANES), :] = h08
        return carry

    lax.fori_loop(0, rows // SUBLANES, per_group_body, 0)

    c = _layernorm(conv_ref[...], lng_ref[...], lnb_ref[...])
    c = _silu(c) * _silu(proj_ref[:, OFF_GC:OFF_GC + CONV_WIDTH])
    mixcr_ref[:, 0:CONV_WIDTH] = c.astype(BF16)

    proj_ref[:, 0:2 * LRU_WIDTH] = jnp.dot(xc_ref[...].astype(BF16), wg_ref[...], preferred_element_type=F32)
    sp = _softplus(-lam_ref[...])
    ba = ba_ref[...]
    bx = bx_ref[...]

    def scan_group(g, carry):
        rs = pl.ds(pl.multiple_of(g * SUBLANES, SUBLANES), SUBLANES)
        a, bt = _lru_coeffs(proj_ref[rs, 0:LRU_WIDTH], proj_ref[rs, LRU_WIDTH:2 * LRU_WIDTH],
                            xc_ref[rs, :], ba, bx, sp)
        a, bt = _group_scan(a, bt, t_new)
        h = a * h0rep_ref[rs, :] + bt
        h_ref[rs, :] = h * _silu(proj_ref[rs, OFF_GR:OFF_GR + LRU_WIDTH])
        for j in range(per_group):
            nh_ref[g * per_group + j] = h[(j + 1) * t_new - 1:(j + 1) * t_new, :]
        return carry

    lax.fori_loop(0, rows // SUBLANES, scan_group, 0)
    mixcr_ref[:, CONV_WIDTH:CONV_WIDTH + LRU_WIDTH] = h_ref[...].astype(BF16)


def _sample_in(l, xs2d, cache_conv, cache_lru_conv, state_lru_h4, p):
    rows = xs2d.shape[0]
    nbatch = cache_conv.shape[1]
    t_new = rows // nbatch
    assert SUBLANES % t_new == 0
    rb = SAMPLE_RB
    rt = rb * t_new
    grid = (nbatch // rb,)
    row_params = [p["norm_pre_g"], p["w_in"], p["conv_w"], p["conv_b"], p["conv_ln_g"], p["conv_ln_b"],
                  p["lru_conv_w"], p["lru_conv_b"], p["w_gate"], p["lru_ba"], p["lru_bx"], p["lru_lambda"]]
    in_specs = [
        pl.BlockSpec((rt, D_MODEL), lambda i: (i, 0)),
        pl.BlockSpec((None, rb, CONV_K - 1, CONV_WIDTH), lambda i: (l, i, 0, 0)),
        pl.BlockSpec((None, rb, LRU_CONV_K - 1, LRU_WIDTH), lambda i: (l, i, 0, 0)),
        pl.BlockSpec((None, rb, 1, LRU_WIDTH), lambda i: (l, i, 0, 0)),
    ] + [_const_spec(a.shape, l) for a in row_params]
    out_shape = [
        jax.ShapeDtypeStruct((rows, CONV_WIDTH + LRU_WIDTH), BF16),
        jax.ShapeDtypeStruct((rows, 2 * MEM_WIDTH), F32),
        jax.ShapeDtypeStruct((nbatch, CONV_K - 1, CONV_WIDTH), F32),
        jax.ShapeDtypeStruct((nbatch, LRU_CONV_K - 1, LRU_WIDTH), F32),
        jax.ShapeDtypeStruct((nbatch, 1, LRU_WIDTH), F32),
    ]
    out_specs = [
        pl.BlockSpec((rt, CONV_WIDTH + LRU_WIDTH), lambda i: (i, 0)),
        pl.BlockSpec((rt, 2 * MEM_WIDTH), lambda i: (i, 0)),
        pl.BlockSpec((rb, CONV_K - 1, CONV_WIDTH), lambda i: (i, 0, 0)),
        pl.BlockSpec((rb, LRU_CONV_K - 1, LRU_WIDTH), lambda i: (i, 0, 0)),
        pl.BlockSpec((rb, 1, LRU_WIDTH), lambda i: (i, 0, 0)),
    ]
    scratch = [
        pltpu.VMEM((rt, IN_WIDTH), F32),
        pltpu.VMEM((rt, CONV_WIDTH), F32),
        pltpu.VMEM((rt, LRU_WIDTH), F32),
        pltpu.VMEM((NCHUNK, CONV_HIST_PAD + 2 * SUBLANES, LANES), F32),
        pltpu.VMEM((NCHUNK, LRU_HIST_PAD + 2 * SUBLANES, LANES), F32),
        pltpu.VMEM((NCHUNK, CONV_K * SUBLANES, LANES), F32),
        pltpu.VMEM((NCHUNK, LRU_CONV_K * SUBLANES, LANES), F32),
        pltpu.VMEM((rt, CONV_WIDTH), F32),
        pltpu.VMEM((rt, LRU_WIDTH), F32),
        pltpu.VMEM((rt, LRU_WIDTH), F32),
        pltpu.VMEM((rt, LRU_WIDTH), F32),
    ]
    return pl.pallas_call(
        _sample_in_kernel,
        grid=grid,
        in_specs=in_specs,
        out_specs=out_specs,
        out_shape=out_shape,
        scratch_shapes=scratch,
        compiler_params=pltpu.CompilerParams(dimension_semantics=("arbitrary",),
                                             vmem_limit_bytes=VMEM_LIMIT),
        name=f"sample_in{l}",
    )(xs2d, cache_conv, cache_lru_conv, state_lru_h4, *row_params)


def _sample_out_kernel(x_ref, mixcr_ref, qg_ref, k_ref, v_ref, wout_ref, gpost_ref, y_ref, mix_ref):
    i = pl.program_id(0)
    nb = k_ref.shape[0]
    rt = qg_ref.shape[0]
    t_new = rt // nb
    per_group = SUBLANES // t_new
    r0 = pl.multiple_of(i * rt, rt)

    lane_head = lax.broadcasted_iota(jnp.int32, (SUBLANES, MEM_WIDTH), 1) // MEM_HEAD_DIM
    row_batch = lax.broadcasted_iota(jnp.int32, (SUBLANES, MEM_WIDTH), 0) // t_new

    outs = []
    for g in range(rt // SUBLANES):
        rs = slice(g * SUBLANES, (g + 1) * SUBLANES)
        q8 = qg_ref[rs, 0:MEM_WIDTH]
        qexp = jnp.concatenate([jnp.where(lane_head == h, q8, 0.0) for h in range(MEM_HEADS)], axis=0)
        qexp = qexp.astype(BF16)
        o8 = jnp.zeros((SUBLANES, MEM_WIDTH), F32)
        for j in range(per_group):
            bi = g * per_group + j
            o_full = _attend(qexp, k_ref[bi].astype(BF16), v_ref[bi].astype(BF16))
            o_b = jnp.zeros((SUBLANES, MEM_WIDTH), F32)
            for h in range(MEM_HEADS):
                o_b = o_b + jnp.where(lane_head == h, o_full[h * SUBLANES:(h + 1) * SUBLANES, :], 0.0)
            o8 = jnp.where(row_batch == j, o_b, o8)
        outs.append(o8 * _silu(qg_ref[rs, MEM_WIDTH:2 * MEM_WIDTH]))
    o = jnp.concatenate(outs, axis=0)
    mix_ref[pl.ds(r0, rt), 0:CONV_WIDTH + LRU_WIDTH] = mixcr_ref[...]
    mix_ref[pl.ds(r0, rt), CONV_WIDTH + LRU_WIDTH:MIX_WIDTH] = o.astype(BF16)

    @pl.when(i == pl.num_programs(0) - 1)
    def _():
        out = jnp.dot(mix_ref[...], wout_ref[...], preferred_element_type=F32)
        y_ref[...] = x_ref[...] + _rmsnorm(out, gpost_ref[...])


def _sample_out(l, xs2d, mixcr, qg, cache_k, cache_v, p):
    rows = xs2d.shape[0]
    nbatch = cache_k.shape[1]
    t_new = rows // nbatch
    ab = SAMPLE_AB
    rt = ab * t_new
    grid = (nbatch // ab,)
    in_specs = [
        pl.BlockSpec((rows, D_MODEL), lambda i: (0, 0)),
        pl.BlockSpec((rt, CONV_WIDTH + LRU_WIDTH), lambda i: (i, 0)),
        pl.BlockSpec((rt, 2 * MEM_WIDTH), lambda i: (i, 0)),
        pl.BlockSpec((None, ab, N_MEM, MEM_WIDTH), lambda i: (l, i, 0, 0)),
        pl.BlockSpec((None, ab, N_MEM, MEM_WIDTH), lambda i: (l, i, 0, 0)),
        _const_spec(p["w_out"].shape, l),
        _const_spec(p["norm_post_g"].shape, l),
    ]
    return pl.pallas_call(
        _sample_out_kernel,
        grid=grid,
        in_specs=in_specs,
        out_specs=pl.BlockSpec((rows, D_MODEL), lambda i: (0, 0)),
        out_shape=jax.ShapeDtypeStruct((rows, D_MODEL), F32),
        scratch_shapes=[pltpu.VMEM((rows, MIX_WIDTH), BF16)],
        compiler_params=pltpu.CompilerParams(dimension_semantics=("arbitrary",),
                                             vmem_limit_bytes=VMEM_LIMIT),
        name=f"sample_out{l}",
    )(xs2d, mixcr, qg, cache_k, cache_v, p["w_out"], p["norm_post_g"])


def _block_diag(w):
    h, d, _ = w.shape
    eye = jnp.eye(h, dtype=w.dtype)
    return (w[:, :, None, :] * eye[:, None, :, None]).reshape(h * d, h * d)


def kernel(x_prompt, x_sample, mem_prompt, cache_conv, cache_lru_conv, state_lru_h, cache_mem_k, cache_mem_v,
           norm_pre_g, w_in, conv_w, conv_b, conv_ln_g, conv_ln_b, lru_conv_w, lru_conv_b, lru_wa, lru_ba,
           lru_wx, lru_bx, lru_lambda, mem_norm_g, w_mem_k, w_mem_v, w_out, norm_post_g):
    depth = w_in.shape[0]
    dec_batch, dec_seq, _ = x_sample.shape

    def vec(a):
        return a.reshape(depth, 1, a.shape[-1])

    w_gate = jnp.concatenate([jax.vmap(_block_diag)(lru_wa), jax.vmap(_block_diag)(lru_wx)], axis=-1)
    params = {
        "norm_pre_g": vec(norm_pre_g), "w_in": w_in.astype(BF16),
        "conv_w": conv_w, "conv_b": vec(conv_b), "conv_ln_g": vec(conv_ln_g), "conv_ln_b": vec(conv_ln_b),
        "lru_conv_w": lru_conv_w, "lru_conv_b": vec(lru_conv_b), "w_gate": w_gate.astype(BF16),
        "lru_ba": vec(lru_ba), "lru_bx": vec(lru_bx), "lru_lambda": vec(lru_lambda),
        "w_out": w_out.astype(BF16), "norm_post_g": vec(norm_post_g),
    }

    p_mk, p_mv = _mem_kv(mem_prompt, mem_norm_g, w_mem_k.astype(BF16), w_mem_v.astype(BF16))
    s_mk = cache_mem_k.reshape(depth, dec_batch, N_MEM, MEM_WIDTH)
    s_mv = cache_mem_v.reshape(depth, dec_batch, N_MEM, MEM_WIDTH)
    s_h0 = state_lru_h.reshape(depth, dec_batch, 1, LRU_WIDTH)

    xp = x_prompt
    xs = x_sample.reshape(dec_batch * dec_seq, D_MODEL)
    p_conv, p_lconv, p_h, s_conv, s_lconv, s_h = [], [], [], [], [], []
    for l in range(depth):
        xp, cb, lb, hh = _prompt_layer(l, xp, p_mk, p_mv, params)
        p_conv.append(cb); p_lconv.append(lb); p_h.append(hh)
        mixcr, qg, cb2, lb2, hh2 = _sample_in(l, xs, cache_conv, cache_lru_conv, s_h0, params)
        xs = _sample_out(l, xs, mixcr, qg, s_mk, s_mv, params)
        s_conv.append(cb2); s_lconv.append(lb2); s_h.append(hh2.reshape(dec_batch, LRU_WIDTH))

    mem_shape = (depth, x_prompt.shape[0], N_MEM, MEM_HEADS, MEM_HEAD_DIM)
    return (xp, xs.reshape(dec_batch, dec_seq, D_MODEL),
            jnp.stack(p_conv), jnp.stack(p_lconv), jnp.stack(p_h),
            p_mk.reshape(mem_shape), p_mv.reshape(mem_shape),
            jnp.stack(s_conv), jnp.stack(s_lconv), jnp.stack(s_h))
```

```python
import math

import jax
import jax.numpy as jnp
from jax import lax
from jax.experimental import pallas as pl
from jax.experimental.pallas import tpu as pltpu

D_MODEL = 1024
MIX_WIDTH = 2048
CONV_WIDTH = 768
LRU_WIDTH = 768
MEM_WIDTH = 512
MEM_HEADS = 4
MEM_HEAD_DIM = 128
N_MEM = 256
CONV_K = 31
LRU_CONV_K = 4
LRU_C = 8.0
EPS = 1e-6
IN_WIDTH = 4864

OFF_A, OFF_B, OFF_GC = 0, 768, 1536
OFF_XR, OFF_GR = 2304, 3072
OFF_Q, OFF_GQ = 3840, 4352

LANES = 128
SUBLANES = 8
NCHUNK = CONV_WIDTH // LANES
CONV_HIST_PAD = 32
LRU_HIST_PAD = 8
PROMPT_TM = 256
CONV_ROWS = 32
SCAN_ROWS = 16
SAMPLE_RB = 32
SAMPLE_AB = 8
VMEM_LIMIT = 56 * 1024 * 1024

BF16 = jnp.bfloat16
F32 = jnp.float32


def _sigmoid(x):
    return 1.0 / (1.0 + jnp.exp(-x))


def _silu(x):
    return x * _sigmoid(x)


def _rmsnorm(x, g):
    return x * lax.rsqrt(jnp.mean(x * x, axis=-1, keepdims=True) + EPS) * g


def _layernorm(x, g, b):
    mu = jnp.mean(x, axis=-1, keepdims=True)
    d = x - mu
    var = jnp.mean(d * d, axis=-1, keepdims=True)
    return d * lax.rsqrt(var + EPS) * g + b


def _softplus(z):
    return jnp.maximum(z, 0.0) + jnp.log1p(jnp.exp(-jnp.abs(z)))


def _lru_coeffs(gate_a, gate_x, xc, ba, bx, sp):
    r = _sigmoid(gate_a + ba)
    ig = _sigmoid(gate_x + bx)
    log_a = (-LRU_C) * r * sp
    a = jnp.exp(log_a)
    mult = jnp.sqrt(jnp.tanh(-log_a) * (1.0 + a * a))
    return a, mult * (ig * xc)


def _group_scan(a, b, period):
    row = lax.broadcasted_iota(jnp.int32, a.shape, 0) % period
    s = 1
    while s < period:
        keep = row >= s
        a_s = pltpu.roll(a, s, 0)
        b_s = pltpu.roll(b, s, 0)
        b = jnp.where(keep, a * b_s + b, b)
        a = jnp.where(keep, a * a_s, a)
        s *= 2
    return a, b


def _attend(q_bf, k_bf, v_bf):
    s = lax.dot_general(q_bf, k_bf, (((1,), (1,)), ((), ())), preferred_element_type=F32)
    s = s * (1.0 / math.sqrt(MEM_HEAD_DIM))
    e = jnp.exp(s - jnp.max(s, axis=-1, keepdims=True))
    l = jnp.sum(e, axis=-1, keepdims=True)
    o = jnp.dot(e.astype(BF16), v_bf, preferred_element_type=F32)
    return o / l


def _store_slabs(dst_ref, row0, val):
    for c in range(NCHUNK):
        dst_ref[c, row0:row0 + val.shape[0], :] = val[:, c * LANES:(c + 1) * LANES]


def _load_slabs(src_ref, row0, nrows):
    return jnp.concatenate([src_ref[c, row0:row0 + nrows, :] for c in range(NCHUNK)], axis=1)


def _store_taps(dst_ref, w):
    for k in range(w.shape[0]):
        for c in range(NCHUNK):
            dst_ref[c, k * SUBLANES:(k + 1) * SUBLANES, :] = jnp.broadcast_to(
                w[k:k + 1, c * LANES:(c + 1) * LANES], (SUBLANES, LANES))


def _window_conv(buf_ref, w_ref, start, taps, ngroups):
    cols = []
    for c in range(NCHUNK):
        accs = [jnp.zeros((SUBLANES, LANES), F32) for _ in range(ngroups)]
        for k in range(taps):
            w8 = w_ref[c, k * SUBLANES:(k + 1) * SUBLANES, :]
            for j in range(ngroups):
                accs[j] = accs[j] + buf_ref[c, pl.ds(start + k + j * SUBLANES, SUBLANES), :] * w8
        cols.append(accs[0] if ngroups == 1 else jnp.concatenate(accs, axis=0))
    return jnp.concatenate(cols, axis=1)


def _mem_kv_kernel(mem_ref, g_ref, wk_ref, wv_ref, kf_ref, vf_ref, kb_ref, vb_ref):
    mn = _rmsnorm(mem_ref[...], g_ref[...]).astype(BF16)
    k = jnp.dot(mn, wk_ref[...], preferred_element_type=F32)
    v = jnp.dot(mn, wv_ref[...], preferred_element_type=F32)
    for h in range(MEM_HEADS):
        cols = slice(h * MEM_HEAD_DIM, (h + 1) * MEM_HEAD_DIM)
        kf_ref[pl.ds(h, N_MEM, stride=MEM_HEADS), :] = k[:, cols]
        vf_ref[pl.ds(h, N_MEM, stride=MEM_HEADS), :] = v[:, cols]
    kb_ref[...] = k.astype(BF16)
    vb_ref[...] = v.astype(BF16)


def _mem_kv(mem_prompt, mem_norm_g, wk_bf, wv_bf):
    depth, batch = wk_bf.shape[0], mem_prompt.shape[0]
    out_f = jax.ShapeDtypeStruct((depth, batch, N_MEM * MEM_HEADS, MEM_HEAD_DIM), F32)
    out_b = jax.ShapeDtypeStruct((depth, batch, N_MEM, MEM_WIDTH), BF16)
    flat_spec = pl.BlockSpec((None, None, N_MEM * MEM_HEADS, MEM_HEAD_DIM), lambda l, b: (l, b, 0, 0))
    return pl.pallas_call(
        _mem_kv_kernel,
        grid=(depth, batch),
        in_specs=[
            pl.BlockSpec((None, N_MEM, D_MODEL), lambda l, b: (b, 0, 0)),
            pl.BlockSpec((None, 1, D_MODEL), lambda l, b: (l, 0, 0)),
            pl.BlockSpec((None, D_MODEL, MEM_WIDTH), lambda l, b: (l, 0, 0)),
            pl.BlockSpec((None, D_MODEL, MEM_WIDTH), lambda l, b: (l, 0, 0)),
        ],
        out_specs=[
            flat_spec, flat_spec,
            pl.BlockSpec((None, None, N_MEM, MEM_WIDTH), lambda l, b: (l, b, 0, 0)),
            pl.BlockSpec((None, None, N_MEM, MEM_WIDTH), lambda l, b: (l, b, 0, 0)),
        ],
        out_shape=[out_f, out_f, out_b, out_b],
        compiler_params=pltpu.CompilerParams(dimension_semantics=("arbitrary", "arbitrary")),
        name="mem_kv",
    )(mem_prompt, mem_norm_g.reshape(depth, 1, D_MODEL), wk_bf, wv_bf)


def _prompt_layer_kernel(x_ref, k_ref, v_ref, gpre_ref, win_ref, cw_ref, cb_ref, lng_ref, lnb_ref,
                         lcw_ref, lcb_ref, wg_ref, ba_ref, bx_ref, lam_ref, wout_ref, gpost_ref,
                         y_ref, nconv_ref, nlru_ref, nh_ref,
                         proj_ref, ubuf_ref, xrbuf_ref, cws_ref, lcws_ref, xc_ref, mix_ref, hcarry_ref):
    b = pl.program_id(0)
    t = pl.program_id(1)
    tm = x_ref.shape[0]

    @pl.when(t == 0)
    def _():
        for c in range(NCHUNK):
            ubuf_ref[c, 0:CONV_HIST_PAD, :] = jnp.zeros((CONV_HIST_PAD, LANES), F32)
            xrbuf_ref[c, 0:LRU_HIST_PAD, :] = jnp.zeros((LRU_HIST_PAD, LANES), F32)
        hcarry_ref[...] = jnp.zeros(hcarry_ref.shape, F32)
        _store_taps(cws_ref, cw_ref[...])
        _store_taps(lcws_ref, lcw_ref[...])

    xn = _rmsnorm(x_ref[...], gpre_ref[...]).astype(BF16)
    proj_ref[...] = jnp.dot(xn, win_ref[...], preferred_element_type=F32)

    _store_slabs(ubuf_ref, CONV_HIST_PAD,
                 proj_ref[:, OFF_A:OFF_A + CONV_WIDTH] * _sigmoid(proj_ref[:, OFF_B:OFF_B + CONV_WIDTH]))
    _store_slabs(xrbuf_ref, LRU_HIST_PAD, proj_ref[:, OFF_XR:OFF_XR + LRU_WIDTH])

    def conv_chunk(i, carry):
        r0 = pl.multiple_of(i * CONV_ROWS, CONV_ROWS)
        acc = _window_conv(ubuf_ref, cws_ref, r0 + (CONV_HIST_PAD - (CONV_K - 1)), CONV_K,
                           CONV_ROWS // SUBLANES)
        c = _layernorm(acc + cb_ref[...], lng_ref[...], lnb_ref[...])
        c = _silu(c) * _silu(proj_ref[pl.ds(r0, CONV_ROWS), OFF_GC:OFF_GC + CONV_WIDTH])
        mix_ref[pl.ds(r0, CONV_ROWS), 0:CONV_WIDTH] = c.astype(BF16)

        xc = _window_conv(xrbuf_ref, lcws_ref, r0 + (LRU_HIST_PAD - (LRU_CONV_K - 1)), LRU_CONV_K,
                          CONV_ROWS // SUBLANES)
        xc_ref[pl.ds(r0, CONV_ROWS), :] = xc + lcb_ref[...]
        return carry

    lax.fori_loop(0, tm // CONV_ROWS, conv_chunk, 0)

    for c in range(NCHUNK):
        ubuf_ref[c, 0:CONV_HIST_PAD, :] = ubuf_ref[c, tm:tm + CONV_HIST_PAD, :]
        xrbuf_ref[c, 0:LRU_HIST_PAD, :] = xrbuf_ref[c, tm:tm + LRU_HIST_PAD, :]

    proj_ref[:, 0:2 * LRU_WIDTH] = jnp.dot(xc_ref[...].astype(BF16), wg_ref[...], preferred_element_type=F32)

    sp = _softplus(-lam_ref[...])
    ba = ba_ref[...]
    bx = bx_ref[...]

    def scan_chunk(i, hprev):
        r0 = pl.multiple_of(i * SCAN_ROWS, SCAN_ROWS)
        hs = []
        for j in range(SCAN_ROWS // SUBLANES):
            rows = pl.ds(r0 + j * SUBLANES, SUBLANES)
            a, bt = _lru_coeffs(proj_ref[rows, 0:LRU_WIDTH], proj_ref[rows, LRU_WIDTH:2 * LRU_WIDTH],
                                xc_ref[rows, :], ba, bx, sp)
            a, bt = _group_scan(a, bt, SUBLANES)
            h = a * hprev + bt
            hprev = jnp.broadcast_to(h[SUBLANES - 1:SUBLANES, :], (SUBLANES, LRU_WIDTH))
            hs.append(h)
        h16 = jnp.concatenate(hs, axis=0)
        rr = h16 * _silu(proj_ref[pl.ds(r0, SCAN_ROWS), OFF_GR:OFF_GR + LRU_WIDTH])
        mix_ref[pl.ds(r0, SCAN_ROWS), CONV_WIDTH:CONV_WIDTH + LRU_WIDTH] = rr.astype(BF16)
        return hprev

    hlast = lax.fori_loop(0, tm // SCAN_ROWS, scan_chunk, hcarry_ref[...])
    hcarry_ref[...] = hlast

    for h in range(MEM_HEADS):
        cols = slice(h * MEM_HEAD_DIM, (h + 1) * MEM_HEAD_DIM)
        q = proj_ref[:, OFF_Q + h * MEM_HEAD_DIM:OFF_Q + (h + 1) * MEM_HEAD_DIM].astype(BF16)
        o = _attend(q, k_ref[:, cols], v_ref[:, cols])
        o = o * _silu(proj_ref[:, OFF_GQ + h * MEM_HEAD_DIM:OFF_GQ + (h + 1) * MEM_HEAD_DIM])
        mix_ref[:, CONV_WIDTH + LRU_WIDTH + h * MEM_HEAD_DIM:
                CONV_WIDTH + LRU_WIDTH + (h + 1) * MEM_HEAD_DIM] = o.astype(BF16)

    out = jnp.dot(mix_ref[...], wout_ref[...], preferred_element_type=F32)
    y_ref[...] = x_ref[...] + _rmsnorm(out, gpost_ref[...])

    @pl.when(t == pl.num_programs(1) - 1)
    def _():
        nconv_ref[...] = _load_slabs(ubuf_ref, CONV_HIST_PAD - (CONV_K - 1), CONV_K - 1)
        nlru_ref[...] = _load_slabs(xrbuf_ref, LRU_HIST_PAD - (LRU_CONV_K - 1), LRU_CONV_K - 1)
        nh_ref[pl.ds(b, 1), :] = hlast[0:1, :]


def _const_spec(shape, l):
    nd = len(shape)
    return pl.BlockSpec((None,) + tuple(shape[1:]), lambda *_: (l,) + (0,) * (nd - 1))


def _prompt_layer(l, x, mk, mv, p):
    batch, seq, _ = x.shape
    tm = PROMPT_TM
    grid = (batch, seq // tm)
    row_params = [p["norm_pre_g"], p["w_in"], p["conv_w"], p["conv_b"], p["conv_ln_g"], p["conv_ln_b"],
                  p["lru_conv_w"], p["lru_conv_b"], p["w_gate"], p["lru_ba"], p["lru_bx"], p["lru_lambda"],
                  p["w_out"], p["norm_post_g"]]
    in_specs = [
        pl.BlockSpec((None, tm, D_MODEL), lambda b, t: (b, t, 0)),
        pl.BlockSpec((None, None, N_MEM, MEM_WIDTH), lambda b, t: (l, b, 0, 0)),
        pl.BlockSpec((None, None, N_MEM, MEM_WIDTH), lambda b, t: (l, b, 0, 0)),
    ] + [_const_spec(a.shape, l) for a in row_params]
    out_shape = [
        jax.ShapeDtypeStruct((batch, seq, D_MODEL), F32),
        jax.ShapeDtypeStruct((batch, CONV_K - 1, CONV_WIDTH), F32),
        jax.ShapeDtypeStruct((batch, LRU_CONV_K - 1, LRU_WIDTH), F32),
        jax.ShapeDtypeStruct((batch, LRU_WIDTH), F32),
    ]
    out_specs = [
        pl.BlockSpec((None, tm, D_MODEL), lambda b, t: (b, t, 0)),
        pl.BlockSpec((None, CONV_K - 1, CONV_WIDTH), lambda b, t: (b, 0, 0)),
        pl.BlockSpec((None, LRU_CONV_K - 1, LRU_WIDTH), lambda b, t: (b, 0, 0)),
        pl.BlockSpec((batch, LRU_WIDTH), lambda b, t: (0, 0)),
    ]
    scratch = [
        pltpu.VMEM((tm, IN_WIDTH), F32),
        pltpu.VMEM((NCHUNK, CONV_HIST_PAD + tm, LANES), F32),
        pltpu.VMEM((NCHUNK, LRU_HIST_PAD + tm, LANES), F32),
        pltpu.VMEM((NCHUNK, CONV_K * SUBLANES, LANES), F32),
        pltpu.VMEM((NCHUNK, LRU_CONV_K * SUBLANES, LANES), F32),
        pltpu.VMEM((tm, LRU_WIDTH), F32),
        pltpu.VMEM((tm, MIX_WIDTH), BF16),
        pltpu.VMEM((SUBLANES, LRU_WIDTH), F32),
    ]
    return pl.pallas_call(
        _prompt_layer_kernel,
        grid=grid,
        in_specs=in_specs,
        out_specs=out_specs,
        out_shape=out_shape,
        scratch_shapes=scratch,
        compiler_params=pltpu.CompilerParams(dimension_semantics=("arbitrary", "arbitrary"),
                                             vmem_limit_bytes=VMEM_LIMIT),
        name=f"prompt_layer{l}",
    )(x, mk, mv, *row_params)


def _sample_in_kernel(x_ref, cconv_ref, clru_ref, h0_ref, gpre_ref, win_ref, cw_ref, cb_ref, lng_ref, lnb_ref,
                      lcw_ref, lcb_ref, wg_ref, ba_ref, bx_ref, lam_ref,
                      mixcr_ref, qg_ref, nconv_ref, nlru_ref, nh_ref,
                      proj_ref, u_ref, xr_ref, cwin_ref, lwin_ref, cws_ref, lcws_ref,
                      conv_ref, xc_ref, h0rep_ref, h_ref):
    rows = x_ref.shape[0]
    nb = cconv_ref.shape[0]
    t_new = rows // nb
    per_group = SUBLANES // t_new
    hist_c = CONV_K - 1
    hist_l = LRU_CONV_K - 1
    row8 = lax.broadcasted_iota(jnp.int32, (SUBLANES, CONV_WIDTH), 0)

    @pl.when(pl.program_id(0) == 0)
    def _():
        _store_taps(cws_ref, cw_ref[...])
        _store_taps(lcws_ref, lcw_ref[...])

    xn = _rmsnorm(x_ref[...], gpre_ref[...]).astype(BF16)
    proj_ref[...] = jnp.dot(xn, win_ref[...], preferred_element_type=F32)
    u_ref[...] = proj_ref[:, OFF_A:OFF_A + CONV_WIDTH] * _sigmoid(proj_ref[:, OFF_B:OFF_B + CONV_WIDTH])
    xr_ref[...] = proj_ref[:, OFF_XR:OFF_XR + LRU_WIDTH]
    qg_ref[...] = proj_ref[:, OFF_Q:OFF_Q + 2 * MEM_WIDTH]

    c_new0 = CONV_HIST_PAD
    l_new0 = LRU_HIST_PAD

    def per_group_body(g, carry):
        r0 = pl.multiple_of(g * SUBLANES, SUBLANES)
        _store_slabs(cwin_ref, c_new0, u_ref[pl.ds(r0, SUBLANES), :])
        _store_slabs(lwin_ref, l_new0, xr_ref[pl.ds(r0, SUBLANES), :])
        conv8 = jnp.zeros((SUBLANES, CONV_WIDTH), F32)
        xc8 = jnp.zeros((SUBLANES, LRU_WIDTH), F32)
        h08 = jnp.zeros((SUBLANES, LRU_WIDTH), F32)
        for j in range(per_group):
            bi = g * per_group + j
            mine = (row8 // t_new) == j
            c0 = c_new0 + j * t_new - hist_c
            _store_slabs(cwin_ref, c0, cconv_ref[bi])
            nconv_ref[bi] = _load_slabs(cwin_ref, c0 + t_new, hist_c)
            acc = _window_conv(cwin_ref, cws_ref, c0, CONV_K, 1)
            conv8 = jnp.where(mine, pltpu.roll(acc, j * t_new, 0) if j else acc, conv8)
            l0 = l_new0 + j * t_new - hist_l
            _store_slabs(lwin_ref, l0, clru_ref[bi])
            nlru_ref[bi] = _load_slabs(lwin_ref, l0 + t_new, hist_l)
            xc = _window_conv(lwin_ref, lcws_ref, l0, LRU_CONV_K, 1)
            xc8 = jnp.where(mine, pltpu.roll(xc, j * t_new, 0) if j else xc, xc8)
            h08 = jnp.where(mine, jnp.broadcast_to(h0_ref[bi], (SUBLANES, LRU_WIDTH)), h08)
        conv_ref[pl.ds(r0, SUBLANES), :] = conv8 + cb_ref[...]
        xc_ref[pl.ds(r0, SUBLANES), :] = xc8 + lcb_ref[...]
        h0rep_ref[pl.ds(r0, SUBLANES), :] = h08
        return carry

    lax.fori_loop(0, rows // SUBLANES, per_group_body, 0)

    c = _layernorm(conv_ref[...], lng_ref[...], lnb_ref[...])
    c = _silu(c) * _silu(proj_ref[:, OFF_GC:OFF_GC + CONV_WIDTH])
    mixcr_ref[:, 0:CONV_WIDTH] = c.astype(BF16)

    proj_ref[:, 0:2 * LRU_WIDTH] = jnp.dot(xc_ref[...].astype(BF16), wg_ref[...], preferred_element_type=F32)
    sp = _softplus(-lam_ref[...])
    ba = ba_ref[...]
    bx = bx_ref[...]

    def scan_group(g, carry):
        rs = pl.ds(pl.multiple_of(g * SUBLANES, SUBLANES), SUBLANES)
        a, bt = _lru_coeffs(proj_ref[rs, 0:LRU_WIDTH], proj_ref[rs, LRU_WIDTH:2 * LRU_WIDTH],
                            xc_ref[rs, :], ba, bx, sp)
        a, bt = _group_scan(a, bt, t_new)
        h = a * h0rep_ref[rs, :] + bt
        h_ref[rs, :] = h * _silu(proj_ref[rs, OFF_GR:OFF_GR + LRU_WIDTH])
        for j in range(per_group):
            nh_ref[g * per_group + j] = h[(j + 1) * t_new - 1:(j + 1) * t_new, :]
        return carry

    lax.fori_loop(0, rows // SUBLANES, scan_group, 0)
    mixcr_ref[:, CONV_WIDTH:CONV_WIDTH + LRU_WIDTH] = h_ref[...].astype(BF16)


def _sample_in(l, xs2d, cache_conv, cache_lru_conv, state_lru_h4, p):
    rows = xs2d.shape[0]
    nbatch = cache_conv.shape[1]
    t_new = rows // nbatch
    assert SUBLANES % t_new == 0
    rb = SAMPLE_RB
    rt = rb * t_new
    grid = (nbatch // rb,)
    row_params = [p["norm_pre_g"], p["w_in"], p["conv_w"], p["conv_b"], p["conv_ln_g"], p["conv_ln_b"],
                  p["lru_conv_w"], p["lru_conv_b"], p["w_gate"], p["lru_ba"], p["lru_bx"], p["lru_lambda"]]
    in_specs = [
        pl.BlockSpec((rt, D_MODEL), lambda i: (i, 0)),
        pl.BlockSpec((None, rb, CONV_K - 1, CONV_WIDTH), lambda i: (l, i, 0, 0)),
        pl.BlockSpec((None, rb, LRU_CONV_K - 1, LRU_WIDTH), lambda i: (l, i, 0, 0)),
        pl.BlockSpec((None, rb, 1, LRU_WIDTH), lambda i: (l, i, 0, 0)),
    ] + [_const_spec(a.shape, l) for a in row_params]
    out_shape = [
        jax.ShapeDtypeStruct((rows, CONV_WIDTH + LRU_WIDTH), BF16),
        jax.ShapeDtypeStruct((rows, 2 * MEM_WIDTH), F32),
        jax.ShapeDtypeStruct((nbatch, CONV_K - 1, CONV_WIDTH), F32),
        jax.ShapeDtypeStruct((nbatch, LRU_CONV_K - 1, LRU_WIDTH), F32),
        jax.ShapeDtypeStruct((nbatch, 1, LRU_WIDTH), F32),
    ]
    out_specs = [
        pl.BlockSpec((rt, CONV_WIDTH + LRU_WIDTH), lambda i: (i, 0)),
        pl.BlockSpec((rt, 2 * MEM_WIDTH), lambda i: (i, 0)),
        pl.BlockSpec((rb, CONV_K - 1, CONV_WIDTH), lambda i: (i, 0, 0)),
        pl.BlockSpec((rb, LRU_CONV_K - 1, LRU_WIDTH), lambda i: (i, 0, 0)),
        pl.BlockSpec((rb, 1, LRU_WIDTH), lambda i: (i, 0, 0)),
    ]
    scratch = [
        pltpu.VMEM((rt, IN_WIDTH), F32),
        pltpu.VMEM((rt, CONV_WIDTH), F32),
        pltpu.VMEM((rt, LRU_WIDTH), F32),
        pltpu.VMEM((NCHUNK, CONV_HIST_PAD + 2 * SUBLANES, LANES), F32),
        pltpu.VMEM((NCHUNK, LRU_HIST_PAD + 2 * SUBLANES, LANES), F32),
        pltpu.VMEM((NCHUNK, CONV_K * SUBLANES, LANES), F32),
        pltpu.VMEM((NCHUNK, LRU_CONV_K * SUBLANES, LANES), F32),
        pltpu.VMEM((rt, CONV_WIDTH), F32),
        pltpu.VMEM((rt, LRU_WIDTH), F32),
        pltpu.VMEM((rt, LRU_WIDTH), F32),
        pltpu.VMEM((rt, LRU_WIDTH), F32),
    ]
    return pl.pallas_call(
        _sample_in_kernel,
        grid=grid,
        in_specs=in_specs,
        out_specs=out_specs,
        out_shape=out_shape,
        scratch_shapes=scratch,
        compiler_params=pltpu.CompilerParams(dimension_semantics=("arbitrary",),
                                             vmem_limit_bytes=VMEM_LIMIT),
        name=f"sample_in{l}",
    )(xs2d, cache_conv, cache_lru_conv, state_lru_h4, *row_params)


def _sample_out_kernel(x_ref, mixcr_ref, qg_ref, k_ref, v_ref, wout_ref, gpost_ref, y_ref, mix_ref):
    i = pl.program_id(0)
    nb = k_ref.shape[0]
    rt = qg_ref.shape[0]
    t_new = rt // nb
    per_group = SUBLANES // t_new
    r0 = pl.multiple_of(i * rt, rt)

    nq = MEM_HEADS * SUBLANES
    nkv = N_MEM * MEM_HEADS
    q_head = lax.broadcasted_iota(jnp.int32, (nq, nkv), 0) // SUBLANES
    kv_head = lax.broadcasted_iota(jnp.int32, (nq, nkv), 1) % MEM_HEADS
    valid = q_head == kv_head
    row_batch = lax.broadcasted_iota(jnp.int32, (SUBLANES, MEM_WIDTH), 0) // t_new
    scale = 1.0 / math.sqrt(MEM_HEAD_DIM)

    outs = []
    for g in range(rt // SUBLANES):
        rs = slice(g * SUBLANES, (g + 1) * SUBLANES)
        q8 = qg_ref[rs, 0:MEM_WIDTH]
        q2 = jnp.concatenate([q8[:, h * MEM_HEAD_DIM:(h + 1) * MEM_HEAD_DIM] for h in range(MEM_HEADS)],
                             axis=0).astype(BF16)
        o8 = jnp.zeros((SUBLANES, MEM_WIDTH), F32)
        for j in range(per_group):
            bi = g * per_group + j
            s = lax.dot_general(q2, k_ref[bi].astype(BF16), (((1,), (1,)), ((), ())),
                                preferred_element_type=F32)
            s = jnp.where(valid, s * scale, -1e30)
            e = jnp.exp(s - jnp.max(s, axis=-1, keepdims=True))
            l = jnp.sum(e, axis=-1, keepdims=True)
            o2 = jnp.dot(e.astype(BF16), v_ref[bi].astype(BF16), preferred_element_type=F32) / l
            o_b = jnp.concatenate([o2[h * SUBLANES:(h + 1) * SUBLANES, :] for h in range(MEM_HEADS)], axis=1)
            o8 = jnp.where(row_batch == j, o_b, o8)
        outs.append(o8 * _silu(qg_ref[rs, MEM_WIDTH:2 * MEM_WIDTH]))
    o = jnp.concatenate(outs, axis=0)
    mix_ref[pl.ds(r0, rt), 0:CONV_WIDTH + LRU_WIDTH] = mixcr_ref[...]
    mix_ref[pl.ds(r0, rt), CONV_WIDTH + LRU_WIDTH:MIX_WIDTH] = o.astype(BF16)

    @pl.when(i == pl.num_programs(0) - 1)
    def _():
        out = jnp.dot(mix_ref[...], wout_ref[...], preferred_element_type=F32)
        y_ref[...] = x_ref[...] + _rmsnorm(out, gpost_ref[...])


def _sample_out(l, xs2d, mixcr, qg, cache_k, cache_v, p):
    rows = xs2d.shape[0]
    nbatch = cache_k.shape[1]
    t_new = rows // nbatch
    ab = SAMPLE_AB
    rt = ab * t_new
    grid = (nbatch // ab,)
    in_specs = [
        pl.BlockSpec((rows, D_MODEL), lambda i: (0, 0)),
        pl.BlockSpec((rt, CONV_WIDTH + LRU_WIDTH), lambda i: (i, 0)),
        pl.BlockSpec((rt, 2 * MEM_WIDTH), lambda i: (i, 0)),
        pl.BlockSpec((None, ab, N_MEM * MEM_HEADS, MEM_HEAD_DIM), lambda i: (l, i, 0, 0)),
        pl.BlockSpec((None, ab, N_MEM * MEM_HEADS, MEM_HEAD_DIM), lambda i: (l, i, 0, 0)),
        _const_spec(p["w_out"].shape, l),
        _const_spec(p["norm_post_g"].shape, l),
    ]
    return pl.pallas_call(
        _sample_out_kernel,
        grid=grid,
        in_specs=in_specs,
        out_specs=pl.BlockSpec((rows, D_MODEL), lambda i: (0, 0)),
        out_shape=jax.ShapeDtypeStruct((rows, D_MODEL), F32),
        scratch_shapes=[pltpu.VMEM((rows, MIX_WIDTH), BF16)],
        compiler_params=pltpu.CompilerParams(dimension_semantics=("arbitrary",),
                                             vmem_limit_bytes=VMEM_LIMIT),
        name=f"sample_out{l}",
    )(xs2d, mixcr, qg, cache_k, cache_v, p["w_out"], p["norm_post_g"])


def _block_diag(w):
    h, d, _ = w.shape
    eye = jnp.eye(h, dtype=w.dtype)
    return (w[:, :, None, :] * eye[:, None, :, None]).reshape(h * d, h * d)


def kernel(x_prompt, x_sample, mem_prompt, cache_conv, cache_lru_conv, state_lru_h, cache_mem_k, cache_mem_v,
           norm_pre_g, w_in, conv_w, conv_b, conv_ln_g, conv_ln_b, lru_conv_w, lru_conv_b, lru_wa, lru_ba,
           lru_wx, lru_bx, lru_lambda, mem_norm_g, w_mem_k, w_mem_v, w_out, norm_post_g):
    depth = w_in.shape[0]
    dec_batch, dec_seq, _ = x_sample.shape

    def vec(a):
        return a.reshape(depth, 1, a.shape[-1])

    w_gate = jnp.concatenate([jax.vmap(_block_diag)(lru_wa), jax.vmap(_block_diag)(lru_wx)], axis=-1)
    params = {
        "norm_pre_g": vec(norm_pre_g), "w_in": w_in.astype(BF16),
        "conv_w": conv_w, "conv_b": vec(conv_b), "conv_ln_g": vec(conv_ln_g), "conv_ln_b": vec(conv_ln_b),
        "lru_conv_w": lru_conv_w, "lru_conv_b": vec(lru_conv_b), "w_gate": w_gate.astype(BF16),
        "lru_ba": vec(lru_ba), "lru_bx": vec(lru_bx), "lru_lambda": vec(lru_lambda),
        "w_out": w_out.astype(BF16), "norm_post_g": vec(norm_post_g),
    }

    p_mk, p_mv, p_mk_bf, p_mv_bf = _mem_kv(mem_prompt, mem_norm_g, w_mem_k.astype(BF16), w_mem_v.astype(BF16))
    s_mk = cache_mem_k.reshape(depth, dec_batch, N_MEM * MEM_HEADS, MEM_HEAD_DIM)
    s_mv = cache_mem_v.reshape(depth, dec_batch, N_MEM * MEM_HEADS, MEM_HEAD_DIM)
    s_h0 = state_lru_h.reshape(depth, dec_batch, 1, LRU_WIDTH)

    xp = x_prompt
    xs = x_sample.reshape(dec_batch * dec_seq, D_MODEL)
    p_conv, p_lconv, p_h, s_conv, s_lconv, s_h = [], [], [], [], [], []
    for l in range(depth):
        xp, cb, lb, hh = _prompt_layer(l, xp, p_mk_bf, p_mv_bf, params)
        p_conv.append(cb); p_lconv.append(lb); p_h.append(hh)
        mixcr, qg, cb2, lb2, hh2 = _sample_in(l, xs, cache_conv, cache_lru_conv, s_h0, params)
        xs = _sample_out(l, xs, mixcr, qg, s_mk, s_mv, params)
        s_conv.append(cb2); s_lconv.append(lb2); s_h.append(hh2.reshape(dec_batch, LRU_WIDTH))

    mem_shape = (depth, x_prompt.shape[0], N_MEM, MEM_HEADS, MEM_HEAD_DIM)
    return (xp, xs.reshape(dec_batch, dec_seq, D_MODEL),
            jnp.stack(p_conv), jnp.stack(p_lconv), jnp.stack(p_h),
            p_mk.reshape(mem_shape), p_mv.reshape(mem_shape),
            jnp.stack(s_conv), jnp.stack(s_lconv), jnp.stack(s_h))
```

```python
import math

import jax
import jax.numpy as jnp
from jax import lax
from jax.experimental import pallas as pl
from jax.experimental.pallas import tpu as pltpu

D_MODEL = 1024
MIX_WIDTH = 2048
CONV_WIDTH = 768
LRU_WIDTH = 768
MEM_WIDTH = 512
MEM_HEADS = 4
MEM_HEAD_DIM = 128
N_MEM = 256
CONV_K = 31
LRU_CONV_K = 4
LRU_C = 8.0
EPS = 1e-6
IN_WIDTH = 4864

OFF_A, OFF_B, OFF_GC = 0, 768, 1536
OFF_XR, OFF_GR = 2304, 3072
OFF_Q, OFF_GQ = 3840, 4352

LANES = 128
SUBLANES = 8
NCHUNK = CONV_WIDTH // LANES
CONV_HIST_PAD = 32
LRU_HIST_PAD = 8
PROMPT_TM = 256
CONV_ROWS = 32
SCAN_ROWS = 16
SAMPLE_RB = 32
SAMPLE_AB = 8
VMEM_LIMIT = 56 * 1024 * 1024

BF16 = jnp.bfloat16
F32 = jnp.float32


LOG2E = 1.4426950408889634


def _sigmoid(x):
    return 1.0 / (1.0 + jnp.exp2(x * (-LOG2E)))


def _silu(x):
    return x * _sigmoid(x)


def _rmsnorm(x, g):
    return x * lax.rsqrt(jnp.mean(x * x, axis=-1, keepdims=True) + EPS) * g


def _layernorm(x, g, b):
    mu = jnp.mean(x, axis=-1, keepdims=True)
    d = x - mu
    var = jnp.mean(d * d, axis=-1, keepdims=True)
    return d * lax.rsqrt(var + EPS) * g + b


def _softplus(z):
    return jnp.maximum(z, 0.0) + jnp.log1p(jnp.exp(-jnp.abs(z)))


def _lru_coeffs(gate_a, gate_x, xc, ba, bx, c_sp):
    r = _sigmoid(gate_a + ba)
    ig = _sigmoid(gate_x + bx)
    neg_log_a = r * c_sp
    a = jnp.exp2(neg_log_a * (-LOG2E))
    y = jnp.tanh(neg_log_a) * (1.0 + a * a)
    mult = jnp.where(y > 0.0, y * lax.rsqrt(y), 0.0)
    return a, mult * (ig * xc)


def _group_scan(a, b, h_in, period):
    first = (lax.broadcasted_iota(jnp.int32, a.shape, 0) % period) == 0
    b = b + jnp.where(first, a * h_in, 0.0)
    a = jnp.where(first, 0.0, a)
    s = 1
    while s < period:
        b = a * pltpu.roll(b, s, 0) + b
        if 2 * s < period:
            a = a * pltpu.roll(a, s, 0)
        s *= 2
    return b


def _attend(q_bf, k_bf, v_bf):
    s = lax.dot_general(q_bf, k_bf, (((1,), (1,)), ((), ())), preferred_element_type=F32)
    s = s * (1.0 / math.sqrt(MEM_HEAD_DIM))
    e = jnp.exp(s - jnp.max(s, axis=-1, keepdims=True))
    l = jnp.sum(e, axis=-1, keepdims=True)
    o = jnp.dot(e.astype(BF16), v_bf, preferred_element_type=F32)
    return o / l


def _store_slabs(dst_ref, row0, val):
    for c in range(NCHUNK):
        dst_ref[c, row0:row0 + val.shape[0], :] = val[:, c * LANES:(c + 1) * LANES]


def _load_slabs(src_ref, row0, nrows):
    return jnp.concatenate([src_ref[c, row0:row0 + nrows, :] for c in range(NCHUNK)], axis=1)


def _store_taps(dst_ref, w):
    for k in range(w.shape[0]):
        for c in range(NCHUNK):
            dst_ref[c, k * SUBLANES:(k + 1) * SUBLANES, :] = jnp.broadcast_to(
                w[k:k + 1, c * LANES:(c + 1) * LANES], (SUBLANES, LANES))


def _window_conv(buf_ref, w_ref, start, taps, ngroups):
    cols = []
    for c in range(NCHUNK):
        accs = [jnp.zeros((SUBLANES, LANES), F32) for _ in range(ngroups)]
        for k in range(taps):
            w8 = w_ref[c, k * SUBLANES:(k + 1) * SUBLANES, :]
            for j in range(ngroups):
                accs[j] = accs[j] + buf_ref[c, pl.ds(start + k + j * SUBLANES, SUBLANES), :] * w8
        cols.append(accs[0] if ngroups == 1 else jnp.concatenate(accs, axis=0))
    return jnp.concatenate(cols, axis=1)


def _mem_kv_kernel(mem_ref, g_ref, wk_ref, wv_ref, kf_ref, vf_ref, kb_ref, vb_ref):
    mn = _rmsnorm(mem_ref[...], g_ref[...]).astype(BF16)
    k = jnp.dot(mn, wk_ref[...], preferred_element_type=F32)
    v = jnp.dot(mn, wv_ref[...], preferred_element_type=F32)
    for h in range(MEM_HEADS):
        cols = slice(h * MEM_HEAD_DIM, (h + 1) * MEM_HEAD_DIM)
        kf_ref[pl.ds(h, N_MEM, stride=MEM_HEADS), :] = k[:, cols]
        vf_ref[pl.ds(h, N_MEM, stride=MEM_HEADS), :] = v[:, cols]
    kb_ref[...] = k.astype(BF16)
    vb_ref[...] = v.astype(BF16)


def _mem_kv(mem_prompt, mem_norm_g, wk_bf, wv_bf):
    depth, batch = wk_bf.shape[0], mem_prompt.shape[0]
    out_f = jax.ShapeDtypeStruct((depth, batch, N_MEM * MEM_HEADS, MEM_HEAD_DIM), F32)
    out_b = jax.ShapeDtypeStruct((depth, batch, N_MEM, MEM_WIDTH), BF16)
    flat_spec = pl.BlockSpec((None, None, N_MEM * MEM_HEADS, MEM_HEAD_DIM), lambda l, b: (l, b, 0, 0))
    return pl.pallas_call(
        _mem_kv_kernel,
        grid=(depth, batch),
        in_specs=[
            pl.BlockSpec((None, N_MEM, D_MODEL), lambda l, b: (b, 0, 0)),
            pl.BlockSpec((None, 1, D_MODEL), lambda l, b: (l, 0, 0)),
            pl.BlockSpec((None, D_MODEL, MEM_WIDTH), lambda l, b: (l, 0, 0)),
            pl.BlockSpec((None, D_MODEL, MEM_WIDTH), lambda l, b: (l, 0, 0)),
        ],
        out_specs=[
            flat_spec, flat_spec,
            pl.BlockSpec((None, None, N_MEM, MEM_WIDTH), lambda l, b: (l, b, 0, 0)),
            pl.BlockSpec((None, None, N_MEM, MEM_WIDTH), lambda l, b: (l, b, 0, 0)),
        ],
        out_shape=[out_f, out_f, out_b, out_b],
        compiler_params=pltpu.CompilerParams(dimension_semantics=("arbitrary", "arbitrary")),
        name="mem_kv",
    )(mem_prompt, mem_norm_g.reshape(depth, 1, D_MODEL), wk_bf, wv_bf)


def _prompt_layer_kernel(x_ref, k_ref, v_ref, gpre_ref, win_ref, cw_ref, cb_ref, lng_ref, lnb_ref,
                         lcw_ref, lcb_ref, wg_ref, ba_ref, bx_ref, lam_ref, wout_ref, gpost_ref,
                         y_ref, nconv_ref, nlru_ref, nh_ref,
                         proj_ref, gate_ref, ubuf_ref, xrbuf_ref, cws_ref, lcws_ref, xc_ref, mix_ref, hcarry_ref):
    b = pl.program_id(0)
    t = pl.program_id(1)
    tm = x_ref.shape[0]

    @pl.when(t == 0)
    def _():
        for c in range(NCHUNK):
            ubuf_ref[c, 0:CONV_HIST_PAD, :] = jnp.zeros((CONV_HIST_PAD, LANES), F32)
            xrbuf_ref[c, 0:LRU_HIST_PAD, :] = jnp.zeros((LRU_HIST_PAD, LANES), F32)
        hcarry_ref[...] = jnp.zeros(hcarry_ref.shape, F32)
        _store_taps(cws_ref, cw_ref[...])
        _store_taps(lcws_ref, lcw_ref[...])

    xn = _rmsnorm(x_ref[...], gpre_ref[...]).astype(BF16)

    def in_proj(c0, c1):
        proj_ref[:, c0:c1] = jnp.dot(xn, win_ref[:, c0:c1], preferred_element_type=F32)

    in_proj(OFF_XR, OFF_GR)
    _store_slabs(xrbuf_ref, LRU_HIST_PAD, proj_ref[:, OFF_XR:OFF_XR + LRU_WIDTH])

    def lru_conv_chunk(i, carry):
        r0 = pl.multiple_of(i * CONV_ROWS, CONV_ROWS)
        xc = _window_conv(xrbuf_ref, lcws_ref, r0 + (LRU_HIST_PAD - (LRU_CONV_K - 1)), LRU_CONV_K,
                          CONV_ROWS // SUBLANES)
        xc_ref[pl.ds(r0, CONV_ROWS), :] = xc + lcb_ref[...]
        return carry

    lax.fori_loop(0, tm // CONV_ROWS, lru_conv_chunk, 0, unroll=True)
    gate_ref[...] = jnp.dot(xc_ref[...].astype(BF16), wg_ref[...], preferred_element_type=F32)

    in_proj(OFF_A, OFF_GC)
    _store_slabs(ubuf_ref, CONV_HIST_PAD,
                 proj_ref[:, OFF_A:OFF_A + CONV_WIDTH] * _sigmoid(proj_ref[:, OFF_B:OFF_B + CONV_WIDTH]))
    in_proj(OFF_GC, OFF_XR)
    in_proj(OFF_GR, IN_WIDTH)

    c_sp = LRU_C * _softplus(-lam_ref[...])
    ba = ba_ref[...]
    bx = bx_ref[...]

    def scan_chunk(i, hprev):
        r0 = pl.multiple_of(i * SCAN_ROWS, SCAN_ROWS)
        hs = []
        for j in range(SCAN_ROWS // SUBLANES):
            rows = pl.ds(r0 + j * SUBLANES, SUBLANES)
            a, bt = _lru_coeffs(gate_ref[rows, 0:LRU_WIDTH], gate_ref[rows, LRU_WIDTH:2 * LRU_WIDTH],
                                xc_ref[rows, :], ba, bx, c_sp)
            h = _group_scan(a, bt, hprev, SUBLANES)
            hprev = jnp.broadcast_to(h[SUBLANES - 1:SUBLANES, :], (SUBLANES, LRU_WIDTH))
            hs.append(h)
        h16 = jnp.concatenate(hs, axis=0)
        rr = h16 * _silu(proj_ref[pl.ds(r0, SCAN_ROWS), OFF_GR:OFF_GR + LRU_WIDTH])
        mix_ref[pl.ds(r0, SCAN_ROWS), CONV_WIDTH:CONV_WIDTH + LRU_WIDTH] = rr.astype(BF16)
        return hprev

    def conv_chunk(i, carry):
        r0 = pl.multiple_of(i * CONV_ROWS, CONV_ROWS)
        acc = _window_conv(ubuf_ref, cws_ref, r0 + (CONV_HIST_PAD - (CONV_K - 1)), CONV_K,
                           CONV_ROWS // SUBLANES)
        c = _layernorm(acc + cb_ref[...], lng_ref[...], lnb_ref[...])
        c = _silu(c) * _silu(proj_ref[pl.ds(r0, CONV_ROWS), OFF_GC:OFF_GC + CONV_WIDTH])
        mix_ref[pl.ds(r0, CONV_ROWS), 0:CONV_WIDTH] = c.astype(BF16)
        return carry

    n_scan = tm // SCAN_ROWS
    n_conv = tm // CONV_ROWS
    hprev = hcarry_ref[...]
    for i in range(n_scan):
        hprev = scan_chunk(i, hprev)
        if (i + 1) % (n_scan // n_conv) == 0:
            conv_chunk((i + 1) // (n_scan // n_conv) - 1, 0)
    hlast = hprev
    hcarry_ref[...] = hlast

    for c in range(NCHUNK):
        ubuf_ref[c, 0:CONV_HIST_PAD, :] = ubuf_ref[c, tm:tm + CONV_HIST_PAD, :]
        xrbuf_ref[c, 0:LRU_HIST_PAD, :] = xrbuf_ref[c, tm:tm + LRU_HIST_PAD, :]

    for h in range(MEM_HEADS):
        cols = slice(h * MEM_HEAD_DIM, (h + 1) * MEM_HEAD_DIM)
        q = proj_ref[:, OFF_Q + h * MEM_HEAD_DIM:OFF_Q + (h + 1) * MEM_HEAD_DIM].astype(BF16)
        o = _attend(q, k_ref[:, cols], v_ref[:, cols])
        o = o * _silu(proj_ref[:, OFF_GQ + h * MEM_HEAD_DIM:OFF_GQ + (h + 1) * MEM_HEAD_DIM])
        mix_ref[:, CONV_WIDTH + LRU_WIDTH + h * MEM_HEAD_DIM:
                CONV_WIDTH + LRU_WIDTH + (h + 1) * MEM_HEAD_DIM] = o.astype(BF16)

    def out_proj(c0, c1):
        return jnp.dot(mix_ref[:, c0:c1], wout_ref[c0:c1, :], preferred_element_type=F32)

    out = (out_proj(0, CONV_WIDTH) + out_proj(CONV_WIDTH + LRU_WIDTH, MIX_WIDTH)
           + out_proj(CONV_WIDTH, CONV_WIDTH + LRU_WIDTH))
    y_ref[...] = x_ref[...] + _rmsnorm(out, gpost_ref[...])

    @pl.when(t == pl.num_programs(1) - 1)
    def _():
        nconv_ref[...] = _load_slabs(ubuf_ref, CONV_HIST_PAD - (CONV_K - 1), CONV_K - 1)
        nlru_ref[...] = _load_slabs(xrbuf_ref, LRU_HIST_PAD - (LRU_CONV_K - 1), LRU_CONV_K - 1)
        nh_ref[pl.ds(b, 1), :] = hlast[0:1, :]


def _const_spec(shape, l):
    nd = len(shape)
    return pl.BlockSpec((None,) + tuple(shape[1:]), lambda *_: (l,) + (0,) * (nd - 1))


def _prompt_layer(l, x, mk, mv, p):
    batch, seq, _ = x.shape
    tm = PROMPT_TM
    grid = (batch, seq // tm)
    row_params = [p["norm_pre_g"], p["w_in"], p["conv_w"], p["conv_b"], p["conv_ln_g"], p["conv_ln_b"],
                  p["lru_conv_w"], p["lru_conv_b"], p["w_gate"], p["lru_ba"], p["lru_bx"], p["lru_lambda"],
                  p["w_out"], p["norm_post_g"]]
    in_specs = [
        pl.BlockSpec((None, tm, D_MODEL), lambda b, t: (b, t, 0)),
        pl.BlockSpec((None, None, N_MEM, MEM_WIDTH), lambda b, t: (l, b, 0, 0)),
        pl.BlockSpec((None, None, N_MEM, MEM_WIDTH), lambda b, t: (l, b, 0, 0)),
    ] + [_const_spec(a.shape, l) for a in row_params]
    out_shape = [
        jax.ShapeDtypeStruct((batch, seq, D_MODEL), F32),
        jax.ShapeDtypeStruct((batch, CONV_K - 1, CONV_WIDTH), F32),
        jax.ShapeDtypeStruct((batch, LRU_CONV_K - 1, LRU_WIDTH), F32),
        jax.ShapeDtypeStruct((batch, LRU_WIDTH), F32),
    ]
    out_specs = [
        pl.BlockSpec((None, tm, D_MODEL), lambda b, t: (b, t, 0)),
        pl.BlockSpec((None, CONV_K - 1, CONV_WIDTH), lambda b, t: (b, 0, 0)),
        pl.BlockSpec((None, LRU_CONV_K - 1, LRU_WIDTH), lambda b, t: (b, 0, 0)),
        pl.BlockSpec((batch, LRU_WIDTH), lambda b, t: (0, 0)),
    ]
    scratch = [
        pltpu.VMEM((tm, IN_WIDTH), F32),
        pltpu.VMEM((tm, 2 * LRU_WIDTH), F32),
        pltpu.VMEM((NCHUNK, CONV_HIST_PAD + tm, LANES), F32),
        pltpu.VMEM((NCHUNK, LRU_HIST_PAD + tm, LANES), F32),
        pltpu.VMEM((NCHUNK, CONV_K * SUBLANES, LANES), F32),
        pltpu.VMEM((NCHUNK, LRU_CONV_K * SUBLANES, LANES), F32),
        pltpu.VMEM((tm, LRU_WIDTH), F32),
        pltpu.VMEM((tm, MIX_WIDTH), BF16),
        pltpu.VMEM((SUBLANES, LRU_WIDTH), F32),
    ]
    return pl.pallas_call(
        _prompt_layer_kernel,
        grid=grid,
        in_specs=in_specs,
        out_specs=out_specs,
        out_shape=out_shape,
        scratch_shapes=scratch,
        compiler_params=pltpu.CompilerParams(dimension_semantics=("arbitrary", "arbitrary"),
                                             vmem_limit_bytes=VMEM_LIMIT),
        name=f"prompt_layer{l}",
    )(x, mk, mv, *row_params)


def _sample_in_kernel(x_ref, cconv_ref, clru_ref, h0_ref, gpre_ref, win_ref, cw_ref, cb_ref, lng_ref, lnb_ref,
                      lcw_ref, lcb_ref, wg_ref, ba_ref, bx_ref, lam_ref,
                      mixcr_ref, qg_ref, nconv_ref, nlru_ref, nh_ref,
                      proj_ref, u_ref, xr_ref, cwin_ref, lwin_ref, cws_ref, lcws_ref,
                      conv_ref, xc_ref, h0rep_ref, h_ref):
    rows = x_ref.shape[0]
    nb = cconv_ref.shape[0]
    t_new = rows // nb
    per_group = SUBLANES // t_new
    hist_c = CONV_K - 1
    hist_l = LRU_CONV_K - 1
    row8 = lax.broadcasted_iota(jnp.int32, (SUBLANES, CONV_WIDTH), 0)

    @pl.when(pl.program_id(0) == 0)
    def _():
        _store_taps(cws_ref, cw_ref[...])
        _store_taps(lcws_ref, lcw_ref[...])

    xn = _rmsnorm(x_ref[...], gpre_ref[...]).astype(BF16)
    proj_ref[...] = jnp.dot(xn, win_ref[...], preferred_element_type=F32)
    u_ref[...] = proj_ref[:, OFF_A:OFF_A + CONV_WIDTH] * _sigmoid(proj_ref[:, OFF_B:OFF_B + CONV_WIDTH])
    xr_ref[...] = proj_ref[:, OFF_XR:OFF_XR + LRU_WIDTH]
    qg_ref[...] = proj_ref[:, OFF_Q:OFF_Q + 2 * MEM_WIDTH]

    c_new0 = CONV_HIST_PAD
    l_new0 = LRU_HIST_PAD

    def per_group_body(g, carry):
        r0 = pl.multiple_of(g * SUBLANES, SUBLANES)
        _store_slabs(cwin_ref, c_new0, u_ref[pl.ds(r0, SUBLANES), :])
        _store_slabs(lwin_ref, l_new0, xr_ref[pl.ds(r0, SUBLANES), :])
        conv8 = jnp.zeros((SUBLANES, CONV_WIDTH), F32)
        xc8 = jnp.zeros((SUBLANES, LRU_WIDTH), F32)
        h08 = jnp.zeros((SUBLANES, LRU_WIDTH), F32)
        for j in range(per_group):
            bi = g * per_group + j
            mine = (row8 // t_new) == j
            c0 = c_new0 + j * t_new - hist_c
            _store_slabs(cwin_ref, c0, cconv_ref[bi])
            nconv_ref[bi] = _load_slabs(cwin_ref, c0 + t_new, hist_c)
            acc = _window_conv(cwin_ref, cws_ref, c0, CONV_K, 1)
            conv8 = jnp.where(mine, pltpu.roll(acc, j * t_new, 0) if j else acc, conv8)
            l0 = l_new0 + j * t_new - hist_l
            _store_slabs(lwin_ref, l0, clru_ref[bi])
            nlru_ref[bi] = _load_slabs(lwin_ref, l0 + t_new, hist_l)
            xc = _window_conv(lwin_ref, lcws_ref, l0, LRU_CONV_K, 1)
            xc8 = jnp.where(mine, pltpu.roll(xc, j * t_new, 0) if j else xc, xc8)
            h08 = jnp.where(mine, jnp.broadcast_to(h0_ref[bi], (SUBLANES, LRU_WIDTH)), h08)
        conv_ref[pl.ds(r0, SUBLANES), :] = conv8 + cb_ref[...]
        xc_ref[pl.ds(r0, SUBLANES), :] = xc8 + lcb_ref[...]
        h0rep_ref[pl.ds(r0, SUBLANES), :] = h08
        return carry

    lax.fori_loop(0, rows // SUBLANES, per_group_body, 0)

    c = _layernorm(conv_ref[...], lng_ref[...], lnb_ref[...])
    c = _silu(c) * _silu(proj_ref[:, OFF_GC:OFF_GC + CONV_WIDTH])
    mixcr_ref[:, 0:CONV_WIDTH] = c.astype(BF16)

    proj_ref[:, 0:2 * LRU_WIDTH] = jnp.dot(xc_ref[...].astype(BF16), wg_ref[...], preferred_element_type=F32)
    c_sp = LRU_C * _softplus(-lam_ref[...])
    ba = ba_ref[...]
    bx = bx_ref[...]

    def scan_group(g, carry):
        rs = pl.ds(pl.multiple_of(g * SUBLANES, SUBLANES), SUBLANES)
        a, bt = _lru_coeffs(proj_ref[rs, 0:LRU_WIDTH], proj_ref[rs, LRU_WIDTH:2 * LRU_WIDTH],
                            xc_ref[rs, :], ba, bx, c_sp)
        h = _group_scan(a, bt, h0rep_ref[rs, :], t_new)
        h_ref[rs, :] = h * _silu(proj_ref[rs, OFF_GR:OFF_GR + LRU_WIDTH])
        for j in range(per_group):
            nh_ref[g * per_group + j] = h[(j + 1) * t_new - 1:(j + 1) * t_new, :]
        return carry

    lax.fori_loop(0, rows // SUBLANES, scan_group, 0)
    mixcr_ref[:, CONV_WIDTH:CONV_WIDTH + LRU_WIDTH] = h_ref[...].astype(BF16)


def _sample_in(l, xs2d, cache_conv, cache_lru_conv, state_lru_h4, p):
    rows = xs2d.shape[0]
    nbatch = cache_conv.shape[1]
    t_new = rows // nbatch
    assert SUBLANES % t_new == 0
    rb = SAMPLE_RB
    rt = rb * t_new
    grid = (nbatch // rb,)
    row_params = [p["norm_pre_g"], p["w_in"], p["conv_w"], p["conv_b"], p["conv_ln_g"], p["conv_ln_b"],
                  p["lru_conv_w"], p["lru_conv_b"], p["w_gate"], p["lru_ba"], p["lru_bx"], p["lru_lambda"]]
    in_specs = [
        pl.BlockSpec((rt, D_MODEL), lambda i: (i, 0)),
        pl.BlockSpec((None, rb, CONV_K - 1, CONV_WIDTH), lambda i: (l, i, 0, 0)),
        pl.BlockSpec((None, rb, LRU_CONV_K - 1, LRU_WIDTH), lambda i: (l, i, 0, 0)),
        pl.BlockSpec((None, rb, 1, LRU_WIDTH), lambda i: (l, i, 0, 0)),
    ] + [_const_spec(a.shape, l) for a in row_params]
    out_shape = [
        jax.ShapeDtypeStruct((rows, CONV_WIDTH + LRU_WIDTH), BF16),
        jax.ShapeDtypeStruct((rows, 2 * MEM_WIDTH), F32),
        jax.ShapeDtypeStruct((nbatch, CONV_K - 1, CONV_WIDTH), F32),
        jax.ShapeDtypeStruct((nbatch, LRU_CONV_K - 1, LRU_WIDTH), F32),
        jax.ShapeDtypeStruct((nbatch, 1, LRU_WIDTH), F32),
    ]
    out_specs = [
        pl.BlockSpec((rt, CONV_WIDTH + LRU_WIDTH), lambda i: (i, 0)),
        pl.BlockSpec((rt, 2 * MEM_WIDTH), lambda i: (i, 0)),
        pl.BlockSpec((rb, CONV_K - 1, CONV_WIDTH), lambda i: (i, 0, 0)),
        pl.BlockSpec((rb, LRU_CONV_K - 1, LRU_WIDTH), lambda i: (i, 0, 0)),
        pl.BlockSpec((rb, 1, LRU_WIDTH), lambda i: (i, 0, 0)),
    ]
    scratch = [
        pltpu.VMEM((rt, IN_WIDTH), F32),
        pltpu.VMEM((rt, CONV_WIDTH), F32),
        pltpu.VMEM((rt, LRU_WIDTH), F32),
        pltpu.VMEM((NCHUNK, CONV_HIST_PAD + 2 * SUBLANES, LANES), F32),
        pltpu.VMEM((NCHUNK, LRU_HIST_PAD + 2 * SUBLANES, LANES), F32),
        pltpu.VMEM((NCHUNK, CONV_K * SUBLANES, LANES), F32),
        pltpu.VMEM((NCHUNK, LRU_CONV_K * SUBLANES, LANES), F32),
        pltpu.VMEM((rt, CONV_WIDTH), F32),
        pltpu.VMEM((rt, LRU_WIDTH), F32),
        pltpu.VMEM((rt, LRU_WIDTH), F32),
        pltpu.VMEM((rt, LRU_WIDTH), F32),
    ]
    return pl.pallas_call(
        _sample_in_kernel,
        grid=grid,
        in_specs=in_specs,
        out_specs=out_specs,
        out_shape=out_shape,
        scratch_shapes=scratch,
        compiler_params=pltpu.CompilerParams(dimension_semantics=("arbitrary",),
                                             vmem_limit_bytes=VMEM_LIMIT),
        name=f"sample_in{l}",
    )(xs2d, cache_conv, cache_lru_conv, state_lru_h4, *row_params)


def _sample_out_kernel(x_ref, mixcr_ref, qg_ref, k_ref, v_ref, wout_ref, gpost_ref, y_ref, mix_ref):
    i = pl.program_id(0)
    nb = k_ref.shape[0]
    rt = qg_ref.shape[0]
    t_new = rt // nb
    per_group = SUBLANES // t_new
    r0 = pl.multiple_of(i * rt, rt)

    nq = MEM_HEADS * SUBLANES
    nkv = N_MEM * MEM_HEADS
    q_head = lax.broadcasted_iota(jnp.int32, (nq, nkv), 0) // SUBLANES
    kv_head = lax.broadcasted_iota(jnp.int32, (nq, nkv), 1) % MEM_HEADS
    valid = q_head == kv_head
    row_batch = lax.broadcasted_iota(jnp.int32, (SUBLANES, MEM_WIDTH), 0) // t_new
    scale = 1.0 / math.sqrt(MEM_HEAD_DIM)

    outs = []
    for g in range(rt // SUBLANES):
        rs = slice(g * SUBLANES, (g + 1) * SUBLANES)
        q8 = qg_ref[rs, 0:MEM_WIDTH]
        q2 = jnp.concatenate([q8[:, h * MEM_HEAD_DIM:(h + 1) * MEM_HEAD_DIM] for h in range(MEM_HEADS)],
                             axis=0).astype(BF16)
        o8 = jnp.zeros((SUBLANES, MEM_WIDTH), F32)
        for j in range(per_group):
            bi = g * per_group + j
            s = lax.dot_general(q2, k_ref[bi].astype(BF16), (((1,), (1,)), ((), ())),
                                preferred_element_type=F32)
            s = jnp.where(valid, s * scale, -1e30)
            e = jnp.exp(s - jnp.max(s, axis=-1, keepdims=True))
            l = jnp.sum(e, axis=-1, keepdims=True)
            o2 = jnp.dot(e.astype(BF16), v_ref[bi].astype(BF16), preferred_element_type=F32) / l
            o_b = jnp.concatenate([o2[h * SUBLANES:(h + 1) * SUBLANES, :] for h in range(MEM_HEADS)], axis=1)
            o8 = jnp.where(row_batch == j, o_b, o8)
        outs.append(o8 * _silu(qg_ref[rs, MEM_WIDTH:2 * MEM_WIDTH]))
    o = jnp.concatenate(outs, axis=0)
    mix_ref[pl.ds(r0, rt), 0:CONV_WIDTH + LRU_WIDTH] = mixcr_ref[...]
    mix_ref[pl.ds(r0, rt), CONV_WIDTH + LRU_WIDTH:MIX_WIDTH] = o.astype(BF16)

    @pl.when(i == pl.num_programs(0) - 1)
    def _():
        out = jnp.dot(mix_ref[...], wout_ref[...], preferred_element_type=F32)
        y_ref[...] = x_ref[...] + _rmsnorm(out, gpost_ref[...])


def _sample_out(l, xs2d, mixcr, qg, cache_k, cache_v, p):
    rows = xs2d.shape[0]
    nbatch = cache_k.shape[1]
    t_new = rows // nbatch
    ab = SAMPLE_AB
    rt = ab * t_new
    grid = (nbatch // ab,)
    in_specs = [
        pl.BlockSpec((rows, D_MODEL), lambda i: (0, 0)),
        pl.BlockSpec((rt, CONV_WIDTH + LRU_WIDTH), lambda i: (i, 0)),
        pl.BlockSpec((rt, 2 * MEM_WIDTH), lambda i: (i, 0)),
        pl.BlockSpec((None, ab, N_MEM * MEM_HEADS, MEM_HEAD_DIM), lambda i: (l, i, 0, 0)),
        pl.BlockSpec((None, ab, N_MEM * MEM_HEADS, MEM_HEAD_DIM), lambda i: (l, i, 0, 0)),
        _const_spec(p["w_out"].shape, l),
        _const_spec(p["norm_post_g"].shape, l),
    ]
    return pl.pallas_call(
        _sample_out_kernel,
        grid=grid,
        in_specs=in_specs,
        out_specs=pl.BlockSpec((rows, D_MODEL), lambda i: (0, 0)),
        out_shape=jax.ShapeDtypeStruct((rows, D_MODEL), F32),
        scratch_shapes=[pltpu.VMEM((rows, MIX_WIDTH), BF16)],
        compiler_params=pltpu.CompilerParams(dimension_semantics=("arbitrary",),
                                             vmem_limit_bytes=VMEM_LIMIT),
        name=f"sample_out{l}",
    )(xs2d, mixcr, qg, cache_k, cache_v, p["w_out"], p["norm_post_g"])


def _block_diag(w):
    h, d, _ = w.shape
    eye = jnp.eye(h, dtype=w.dtype)
    return (w[:, :, None, :] * eye[:, None, :, None]).reshape(h * d, h * d)


def kernel(x_prompt, x_sample, mem_prompt, cache_conv, cache_lru_conv, state_lru_h, cache_mem_k, cache_mem_v,
           norm_pre_g, w_in, conv_w, conv_b, conv_ln_g, conv_ln_b, lru_conv_w, lru_conv_b, lru_wa, lru_ba,
           lru_wx, lru_bx, lru_lambda, mem_norm_g, w_mem_k, w_mem_v, w_out, norm_post_g):
    depth = w_in.shape[0]
    dec_batch, dec_seq, _ = x_sample.shape

    def vec(a):
        return a.reshape(depth, 1, a.shape[-1])

    w_gate = jnp.concatenate([jax.vmap(_block_diag)(lru_wa), jax.vmap(_block_diag)(lru_wx)], axis=-1)
    params = {
        "norm_pre_g": vec(norm_pre_g), "w_in": w_in.astype(BF16),
        "conv_w": conv_w, "conv_b": vec(conv_b), "conv_ln_g": vec(conv_ln_g), "conv_ln_b": vec(conv_ln_b),
        "lru_conv_w": lru_conv_w, "lru_conv_b": vec(lru_conv_b), "w_gate": w_gate.astype(BF16),
        "lru_ba": vec(lru_ba), "lru_bx": vec(lru_bx), "lru_lambda": vec(lru_lambda),
        "w_out": w_out.astype(BF16), "norm_post_g": vec(norm_post_g),
    }

    p_mk, p_mv, p_mk_bf, p_mv_bf = _mem_kv(mem_prompt, mem_norm_g, w_mem_k.astype(BF16), w_mem_v.astype(BF16))
    s_mk = cache_mem_k.reshape(depth, dec_batch, N_MEM * MEM_HEADS, MEM_HEAD_DIM)
    s_mv = cache_mem_v.reshape(depth, dec_batch, N_MEM * MEM_HEADS, MEM_HEAD_DIM)
    s_h0 = state_lru_h.reshape(depth, dec_batch, 1, LRU_WIDTH)

    xp = x_prompt
    xs = x_sample.reshape(dec_batch * dec_seq, D_MODEL)
    p_conv, p_lconv, p_h, s_conv, s_lconv, s_h = [], [], [], [], [], []
    for l in range(depth):
        xp, cb, lb, hh = _prompt_layer(l, xp, p_mk_bf, p_mv_bf, params)
        p_conv.append(cb); p_lconv.append(lb); p_h.append(hh)
        mixcr, qg, cb2, lb2, hh2 = _sample_in(l, xs, cache_conv, cache_lru_conv, s_h0, params)
        xs = _sample_out(l, xs, mixcr, qg, s_mk, s_mv, params)
        s_conv.append(cb2); s_lconv.append(lb2); s_h.append(hh2.reshape(dec_batch, LRU_WIDTH))

    mem_shape = (depth, x_prompt.shape[0], N_MEM, MEM_HEADS, MEM_HEAD_DIM)
    return (xp, xs.reshape(dec_batch, dec_seq, D_MODEL),
            jnp.stack(p_conv), jnp.stack(p_lconv), jnp.stack(p_h),
            p_mk.reshape(mem_shape), p_mv.reshape(mem_shape),
            jnp.stack(s_conv), jnp.stack(s_lconv), jnp.stack(s_h))
```

```python
import math

import jax
import jax.numpy as jnp
from jax import lax
from jax.experimental import pallas as pl
from jax.experimental.pallas import tpu as pltpu

D_MODEL = 1024
MIX_WIDTH = 2048
CONV_WIDTH = 768
LRU_WIDTH = 768
MEM_WIDTH = 512
MEM_HEADS = 4
MEM_HEAD_DIM = 128
N_MEM = 256
CONV_K = 31
LRU_CONV_K = 4
LRU_C = 8.0
EPS = 1e-6
IN_WIDTH = 4864

OFF_A, OFF_B, OFF_GC = 0, 768, 1536
OFF_XR, OFF_GR = 2304, 3072
OFF_Q, OFF_GQ = 3840, 4352

LANES = 128
SUBLANES = 8
NCHUNK = CONV_WIDTH // LANES
CONV_HIST_PAD = 32
LRU_HIST_PAD = 8
PROMPT_TM = 512
CONV_ROWS = 32
SCAN_ROWS = 16
SAMPLE_RB = 32
SAMPLE_AB = 8
VMEM_LIMIT = 56 * 1024 * 1024

BF16 = jnp.bfloat16
F32 = jnp.float32


LOG2E = 1.4426950408889634


def _sigmoid(x):
    return 1.0 / (1.0 + jnp.exp2(x * (-LOG2E)))


def _silu(x):
    return x * _sigmoid(x)


def _rmsnorm(x, g):
    return x * lax.rsqrt(jnp.mean(x * x, axis=-1, keepdims=True) + EPS) * g


def _layernorm(x, g, b):
    mu = jnp.mean(x, axis=-1, keepdims=True)
    d = x - mu
    var = jnp.mean(d * d, axis=-1, keepdims=True)
    return d * lax.rsqrt(var + EPS) * g + b


def _softplus(z):
    return jnp.maximum(z, 0.0) + jnp.log1p(jnp.exp(-jnp.abs(z)))


def _lru_coeffs(gate_a, gate_x, xc, ba, bx, c_sp):
    r = _sigmoid(gate_a + ba)
    ig = _sigmoid(gate_x + bx)
    neg_log_a = r * c_sp
    a = jnp.exp2(neg_log_a * (-LOG2E))
    y = jnp.tanh(neg_log_a) * (1.0 + a * a)
    mult = jnp.where(y > 0.0, y * lax.rsqrt(y), 0.0)
    return a, mult * (ig * xc)


def _group_scan(a, b, h_in, period):
    first = (lax.broadcasted_iota(jnp.int32, a.shape, 0) % period) == 0
    b = b + jnp.where(first, a * h_in, 0.0)
    a = jnp.where(first, 0.0, a)
    s = 1
    while s < period:
        b = a * pltpu.roll(b, s, 0) + b
        if 2 * s < period:
            a = a * pltpu.roll(a, s, 0)
        s *= 2
    return b


def _attend(q_bf, k_bf, v_bf):
    s = lax.dot_general(q_bf, k_bf, (((1,), (1,)), ((), ())), preferred_element_type=F32)
    s = s * (1.0 / math.sqrt(MEM_HEAD_DIM))
    e = jnp.exp(s - jnp.max(s, axis=-1, keepdims=True))
    l = jnp.sum(e, axis=-1, keepdims=True)
    o = jnp.dot(e.astype(BF16), v_bf, preferred_element_type=F32)
    return o / l


def _store_slabs(dst_ref, row0, val):
    for c in range(NCHUNK):
        dst_ref[c, row0:row0 + val.shape[0], :] = val[:, c * LANES:(c + 1) * LANES]


def _load_slabs(src_ref, row0, nrows):
    return jnp.concatenate([src_ref[c, row0:row0 + nrows, :] for c in range(NCHUNK)], axis=1)


def _store_taps(dst_ref, w):
    for k in range(w.shape[0]):
        for c in range(NCHUNK):
            dst_ref[c, k * SUBLANES:(k + 1) * SUBLANES, :] = jnp.broadcast_to(
                w[k:k + 1, c * LANES:(c + 1) * LANES], (SUBLANES, LANES))


def _window_conv(buf_ref, w_ref, start, taps, ngroups):
    cols = []
    for c in range(NCHUNK):
        accs = [jnp.zeros((SUBLANES, LANES), F32) for _ in range(ngroups)]
        for k in range(taps):
            w8 = w_ref[c, k * SUBLANES:(k + 1) * SUBLANES, :]
            for j in range(ngroups):
                accs[j] = accs[j] + buf_ref[c, pl.ds(start + k + j * SUBLANES, SUBLANES), :] * w8
        cols.append(accs[0] if ngroups == 1 else jnp.concatenate(accs, axis=0))
    return jnp.concatenate(cols, axis=1)


def _mem_kv_kernel(mem_ref, g_ref, wk_ref, wv_ref, kf_ref, vf_ref, kb_ref, vb_ref):
    mn = _rmsnorm(mem_ref[...], g_ref[...]).astype(BF16)
    k = jnp.dot(mn, wk_ref[...], preferred_element_type=F32)
    v = jnp.dot(mn, wv_ref[...], preferred_element_type=F32)
    for h in range(MEM_HEADS):
        cols = slice(h * MEM_HEAD_DIM, (h + 1) * MEM_HEAD_DIM)
        kf_ref[pl.ds(h, N_MEM, stride=MEM_HEADS), :] = k[:, cols]
        vf_ref[pl.ds(h, N_MEM, stride=MEM_HEADS), :] = v[:, cols]
    kb_ref[...] = k.astype(BF16)
    vb_ref[...] = v.astype(BF16)


def _mem_kv(mem_prompt, mem_norm_g, wk_bf, wv_bf):
    depth, batch = wk_bf.shape[0], mem_prompt.shape[0]
    out_f = jax.ShapeDtypeStruct((depth, batch, N_MEM * MEM_HEADS, MEM_HEAD_DIM), F32)
    out_b = jax.ShapeDtypeStruct((depth, batch, N_MEM, MEM_WIDTH), BF16)
    flat_spec = pl.BlockSpec((None, None, N_MEM * MEM_HEADS, MEM_HEAD_DIM), lambda l, b: (l, b, 0, 0))
    return pl.pallas_call(
        _mem_kv_kernel,
        grid=(depth, batch),
        in_specs=[
            pl.BlockSpec((None, N_MEM, D_MODEL), lambda l, b: (b, 0, 0)),
            pl.BlockSpec((None, 1, D_MODEL), lambda l, b: (l, 0, 0)),
            pl.BlockSpec((None, D_MODEL, MEM_WIDTH), lambda l, b: (l, 0, 0)),
            pl.BlockSpec((None, D_MODEL, MEM_WIDTH), lambda l, b: (l, 0, 0)),
        ],
        out_specs=[
            flat_spec, flat_spec,
            pl.BlockSpec((None, None, N_MEM, MEM_WIDTH), lambda l, b: (l, b, 0, 0)),
            pl.BlockSpec((None, None, N_MEM, MEM_WIDTH), lambda l, b: (l, b, 0, 0)),
        ],
        out_shape=[out_f, out_f, out_b, out_b],
        compiler_params=pltpu.CompilerParams(dimension_semantics=("arbitrary", "arbitrary")),
        name="mem_kv",
    )(mem_prompt, mem_norm_g.reshape(depth, 1, D_MODEL), wk_bf, wv_bf)


def _prompt_layer_kernel(x_ref, k_ref, v_ref, gpre_ref, win_ref, cw_ref, cb_ref, lng_ref, lnb_ref,
                         lcw_ref, lcb_ref, wg_ref, ba_ref, bx_ref, lam_ref, wout_ref, gpost_ref,
                         y_ref, nconv_ref, nlru_ref, nh_ref,
                         proj_ref, gate_ref, ubuf_ref, xrbuf_ref, cws_ref, lcws_ref, xc_ref, mix_ref, hcarry_ref):
    b = pl.program_id(0)
    t = pl.program_id(1)
    tm = x_ref.shape[0]

    @pl.when(t == 0)
    def _():
        for c in range(NCHUNK):
            ubuf_ref[c, 0:CONV_HIST_PAD, :] = jnp.zeros((CONV_HIST_PAD, LANES), F32)
            xrbuf_ref[c, 0:LRU_HIST_PAD, :] = jnp.zeros((LRU_HIST_PAD, LANES), F32)
        hcarry_ref[...] = jnp.zeros(hcarry_ref.shape, F32)
        _store_taps(cws_ref, cw_ref[...])
        _store_taps(lcws_ref, lcw_ref[...])

    xn = _rmsnorm(x_ref[...], gpre_ref[...]).astype(BF16)

    def in_proj(c0, c1):
        proj_ref[:, c0:c1] = jnp.dot(xn, win_ref[:, c0:c1], preferred_element_type=F32)

    in_proj(OFF_XR, OFF_GR)
    _store_slabs(xrbuf_ref, LRU_HIST_PAD, proj_ref[:, OFF_XR:OFF_XR + LRU_WIDTH])

    def lru_conv_chunk(i, carry):
        r0 = pl.multiple_of(i * CONV_ROWS, CONV_ROWS)
        xc = _window_conv(xrbuf_ref, lcws_ref, r0 + (LRU_HIST_PAD - (LRU_CONV_K - 1)), LRU_CONV_K,
                          CONV_ROWS // SUBLANES)
        xc_ref[pl.ds(r0, CONV_ROWS), :] = xc + lcb_ref[...]
        return carry

    lax.fori_loop(0, tm // CONV_ROWS, lru_conv_chunk, 0, unroll=True)
    gate_ref[...] = jnp.dot(xc_ref[...].astype(BF16), wg_ref[...], preferred_element_type=F32)

    in_proj(OFF_A, OFF_GC)
    _store_slabs(ubuf_ref, CONV_HIST_PAD,
                 proj_ref[:, OFF_A:OFF_A + CONV_WIDTH] * _sigmoid(proj_ref[:, OFF_B:OFF_B + CONV_WIDTH]))
    in_proj(OFF_GC, OFF_XR)
    in_proj(OFF_GR, IN_WIDTH)

    c_sp = LRU_C * _softplus(-lam_ref[...])
    ba = ba_ref[...]
    bx = bx_ref[...]

    def scan_chunk(i, hprev):
        r0 = pl.multiple_of(i * SCAN_ROWS, SCAN_ROWS)
        hs = []
        for j in range(SCAN_ROWS // SUBLANES):
            rows = pl.ds(r0 + j * SUBLANES, SUBLANES)
            a, bt = _lru_coeffs(gate_ref[rows, 0:LRU_WIDTH], gate_ref[rows, LRU_WIDTH:2 * LRU_WIDTH],
                                xc_ref[rows, :], ba, bx, c_sp)
            h = _group_scan(a, bt, hprev, SUBLANES)
            hprev = jnp.broadcast_to(h[SUBLANES - 1:SUBLANES, :], (SUBLANES, LRU_WIDTH))
            hs.append(h)
        h16 = jnp.concatenate(hs, axis=0)
        rr = h16 * _silu(proj_ref[pl.ds(r0, SCAN_ROWS), OFF_GR:OFF_GR + LRU_WIDTH])
        mix_ref[pl.ds(r0, SCAN_ROWS), CONV_WIDTH:CONV_WIDTH + LRU_WIDTH] = rr.astype(BF16)
        return hprev

    def conv_chunk(i, carry):
        r0 = pl.multiple_of(i * CONV_ROWS, CONV_ROWS)
        acc = _window_conv(ubuf_ref, cws_ref, r0 + (CONV_HIST_PAD - (CONV_K - 1)), CONV_K,
                           CONV_ROWS // SUBLANES)
        c = _layernorm(acc + cb_ref[...], lng_ref[...], lnb_ref[...])
        c = _silu(c) * _silu(proj_ref[pl.ds(r0, CONV_ROWS), OFF_GC:OFF_GC + CONV_WIDTH])
        mix_ref[pl.ds(r0, CONV_ROWS), 0:CONV_WIDTH] = c.astype(BF16)
        return carry

    n_scan = tm // SCAN_ROWS
    n_conv = tm // CONV_ROWS
    hprev = hcarry_ref[...]
    for i in range(n_scan):
        hprev = scan_chunk(i, hprev)
        if (i + 1) % (n_scan // n_conv) == 0:
            conv_chunk((i + 1) // (n_scan // n_conv) - 1, 0)
    hlast = hprev
    hcarry_ref[...] = hlast

    for c in range(NCHUNK):
        ubuf_ref[c, 0:CONV_HIST_PAD, :] = ubuf_ref[c, tm:tm + CONV_HIST_PAD, :]
        xrbuf_ref[c, 0:LRU_HIST_PAD, :] = xrbuf_ref[c, tm:tm + LRU_HIST_PAD, :]

    for h in range(MEM_HEADS):
        cols = slice(h * MEM_HEAD_DIM, (h + 1) * MEM_HEAD_DIM)
        q = proj_ref[:, OFF_Q + h * MEM_HEAD_DIM:OFF_Q + (h + 1) * MEM_HEAD_DIM].astype(BF16)
        o = _attend(q, k_ref[:, cols], v_ref[:, cols])
        o = o * _silu(proj_ref[:, OFF_GQ + h * MEM_HEAD_DIM:OFF_GQ + (h + 1) * MEM_HEAD_DIM])
        mix_ref[:, CONV_WIDTH + LRU_WIDTH + h * MEM_HEAD_DIM:
                CONV_WIDTH + LRU_WIDTH + (h + 1) * MEM_HEAD_DIM] = o.astype(BF16)

    def out_proj(c0, c1):
        return jnp.dot(mix_ref[:, c0:c1], wout_ref[c0:c1, :], preferred_element_type=F32)

    out = (out_proj(0, CONV_WIDTH) + out_proj(CONV_WIDTH + LRU_WIDTH, MIX_WIDTH)
           + out_proj(CONV_WIDTH, CONV_WIDTH + LRU_WIDTH))
    y_ref[...] = x_ref[...] + _rmsnorm(out, gpost_ref[...])

    @pl.when(t == pl.num_programs(1) - 1)
    def _():
        nconv_ref[...] = _load_slabs(ubuf_ref, CONV_HIST_PAD - (CONV_K - 1), CONV_K - 1)
        nlru_ref[...] = _load_slabs(xrbuf_ref, LRU_HIST_PAD - (LRU_CONV_K - 1), LRU_CONV_K - 1)
        nh_ref[pl.ds(b, 1), :] = hlast[0:1, :]


def _const_spec(shape, l):
    nd = len(shape)
    return pl.BlockSpec((None,) + tuple(shape[1:]), lambda *_: (l,) + (0,) * (nd - 1))


def _prompt_layer(l, x, mk, mv, p):
    batch, seq, _ = x.shape
    tm = PROMPT_TM
    grid = (batch, seq // tm)
    row_params = [p["norm_pre_g"], p["w_in"], p["conv_w"], p["conv_b"], p["conv_ln_g"], p["conv_ln_b"],
                  p["lru_conv_w"], p["lru_conv_b"], p["w_gate"], p["lru_ba"], p["lru_bx"], p["lru_lambda"],
                  p["w_out"], p["norm_post_g"]]
    in_specs = [
        pl.BlockSpec((None, tm, D_MODEL), lambda b, t: (b, t, 0)),
        pl.BlockSpec((None, None, N_MEM, MEM_WIDTH), lambda b, t: (l, b, 0, 0)),
        pl.BlockSpec((None, None, N_MEM, MEM_WIDTH), lambda b, t: (l, b, 0, 0)),
    ] + [_const_spec(a.shape, l) for a in row_params]
    out_shape = [
        jax.ShapeDtypeStruct((batch, seq, D_MODEL), F32),
        jax.ShapeDtypeStruct((batch, CONV_K - 1, CONV_WIDTH), F32),
        jax.ShapeDtypeStruct((batch, LRU_CONV_K - 1, LRU_WIDTH), F32),
        jax.ShapeDtypeStruct((batch, LRU_WIDTH), F32),
    ]
    out_specs = [
        pl.BlockSpec((None, tm, D_MODEL), lambda b, t: (b, t, 0)),
        pl.BlockSpec((None, CONV_K - 1, CONV_WIDTH), lambda b, t: (b, 0, 0)),
        pl.BlockSpec((None, LRU_CONV_K - 1, LRU_WIDTH), lambda b, t: (b, 0, 0)),
        pl.BlockSpec((batch, LRU_WIDTH), lambda b, t: (0, 0)),
    ]
    scratch = [
        pltpu.VMEM((tm, IN_WIDTH), F32),
        pltpu.VMEM((tm, 2 * LRU_WIDTH), F32),
        pltpu.VMEM((NCHUNK, CONV_HIST_PAD + tm, LANES), F32),
        pltpu.VMEM((NCHUNK, LRU_HIST_PAD + tm, LANES), F32),
        pltpu.VMEM((NCHUNK, CONV_K * SUBLANES, LANES), F32),
        pltpu.VMEM((NCHUNK, LRU_CONV_K * SUBLANES, LANES), F32),
        pltpu.VMEM((tm, LRU_WIDTH), F32),
        pltpu.VMEM((tm, MIX_WIDTH), BF16),
        pltpu.VMEM((SUBLANES, LRU_WIDTH), F32),
    ]
    return pl.pallas_call(
        _prompt_layer_kernel,
        grid=grid,
        in_specs=in_specs,
        out_specs=out_specs,
        out_shape=out_shape,
        scratch_shapes=scratch,
        compiler_params=pltpu.CompilerParams(dimension_semantics=("arbitrary", "arbitrary"),
                                             vmem_limit_bytes=VMEM_LIMIT),
        name=f"prompt_layer{l}",
    )(x, mk, mv, *row_params)


def _sample_in_kernel(x_ref, cconv_ref, clru_ref, h0_ref, gpre_ref, win_ref, cw_ref, cb_ref, lng_ref, lnb_ref,
                      lcw_ref, lcb_ref, wg_ref, ba_ref, bx_ref, lam_ref, nconv_all_ref,
                      mixcr_ref, qg_ref, nconv_ref, nlru_ref, nh_ref,
                      proj_ref, u_ref, xr_ref, cwin_ref, lwin_ref, cws_ref, lcws_ref,
                      conv_ref, xc_ref, h0rep_ref, h_ref):
    rows = x_ref.shape[0]
    nb = cconv_ref.shape[0]
    t_new = rows // nb
    per_group = SUBLANES // t_new
    hist_c = CONV_K - 1
    hist_l = LRU_CONV_K - 1
    row8 = lax.broadcasted_iota(jnp.int32, (SUBLANES, CONV_WIDTH), 0)

    @pl.when(pl.program_id(0) == 0)
    def _():
        _store_taps(cws_ref, cw_ref[...])
        _store_taps(lcws_ref, lcw_ref[...])

    xn = _rmsnorm(x_ref[...], gpre_ref[...]).astype(BF16)
    proj_ref[...] = jnp.dot(xn, win_ref[...], preferred_element_type=F32)
    u_ref[...] = proj_ref[:, OFF_A:OFF_A + CONV_WIDTH] * _sigmoid(proj_ref[:, OFF_B:OFF_B + CONV_WIDTH])
    xr_ref[...] = proj_ref[:, OFF_XR:OFF_XR + LRU_WIDTH]
    qg_ref[...] = proj_ref[:, OFF_Q:OFF_Q + 2 * MEM_WIDTH]

    c_new0 = CONV_HIST_PAD
    l_new0 = LRU_HIST_PAD

    def per_group_body(g, carry):
        r0 = pl.multiple_of(g * SUBLANES, SUBLANES)
        _store_slabs(cwin_ref, c_new0, u_ref[pl.ds(r0, SUBLANES), :])
        _store_slabs(lwin_ref, l_new0, xr_ref[pl.ds(r0, SUBLANES), :])
        conv8 = jnp.zeros((SUBLANES, CONV_WIDTH), F32)
        xc8 = jnp.zeros((SUBLANES, LRU_WIDTH), F32)
        h08 = jnp.zeros((SUBLANES, LRU_WIDTH), F32)
        for j in range(per_group):
            bi = g * per_group + j
            mine = (row8 // t_new) == j
            c0 = c_new0 + j * t_new - hist_c
            _store_slabs(cwin_ref, c0, cconv_ref[bi])
            nconv_ref[bi] = _load_slabs(cwin_ref, c0 + t_new, hist_c)
            acc = _window_conv(cwin_ref, cws_ref, c0, CONV_K, 1)
            conv8 = jnp.where(mine, pltpu.roll(acc, j * t_new, 0) if j else acc, conv8)
            l0 = l_new0 + j * t_new - hist_l
            _store_slabs(lwin_ref, l0, clru_ref[bi])
            nlru_ref[bi] = _load_slabs(lwin_ref, l0 + t_new, hist_l)
            xc = _window_conv(lwin_ref, lcws_ref, l0, LRU_CONV_K, 1)
            xc8 = jnp.where(mine, pltpu.roll(xc, j * t_new, 0) if j else xc, xc8)
            h08 = jnp.where(mine, jnp.broadcast_to(h0_ref[bi], (SUBLANES, LRU_WIDTH)), h08)
        conv_ref[pl.ds(r0, SUBLANES), :] = conv8 + cb_ref[...]
        xc_ref[pl.ds(r0, SUBLANES), :] = xc8 + lcb_ref[...]
        h0rep_ref[pl.ds(r0, SUBLANES), :] = h08
        return carry

    lax.fori_loop(0, rows // SUBLANES, per_group_body, 0)

    c = _layernorm(conv_ref[...], lng_ref[...], lnb_ref[...])
    c = _silu(c) * _silu(proj_ref[:, OFF_GC:OFF_GC + CONV_WIDTH])
    mixcr_ref[:, 0:CONV_WIDTH] = c.astype(BF16)

    proj_ref[:, 0:2 * LRU_WIDTH] = jnp.dot(xc_ref[...].astype(BF16), wg_ref[...], preferred_element_type=F32)
    c_sp = LRU_C * _softplus(-lam_ref[...])
    ba = ba_ref[...]
    bx = bx_ref[...]

    def scan_group(g, carry):
        rs = pl.ds(pl.multiple_of(g * SUBLANES, SUBLANES), SUBLANES)
        a, bt = _lru_coeffs(proj_ref[rs, 0:LRU_WIDTH], proj_ref[rs, LRU_WIDTH:2 * LRU_WIDTH],
                            xc_ref[rs, :], ba, bx, c_sp)
        h = _group_scan(a, bt, h0rep_ref[rs, :], t_new)
        h_ref[rs, :] = h * _silu(proj_ref[rs, OFF_GR:OFF_GR + LRU_WIDTH])
        for j in range(per_group):
            nh_ref[g * per_group + j] = h[(j + 1) * t_new - 1:(j + 1) * t_new, :]
        return carry

    lax.fori_loop(0, rows // SUBLANES, scan_group, 0)
    mixcr_ref[:, CONV_WIDTH:CONV_WIDTH + LRU_WIDTH] = h_ref[...].astype(BF16)


def _sample_in(l, xs2d, cache_conv, cache_lru_conv, state_lru_h4, nconv_all, p):
    rows = xs2d.shape[0]
    depth, nbatch = cache_conv.shape[0], cache_conv.shape[1]
    t_new = rows // nbatch
    assert SUBLANES % t_new == 0
    rb = SAMPLE_RB
    rt = rb * t_new
    grid = (nbatch // rb,)
    row_params = [p["norm_pre_g"], p["w_in"], p["conv_w"], p["conv_b"], p["conv_ln_g"], p["conv_ln_b"],
                  p["lru_conv_w"], p["lru_conv_b"], p["w_gate"], p["lru_ba"], p["lru_bx"], p["lru_lambda"]]
    in_specs = [
        pl.BlockSpec((rt, D_MODEL), lambda i: (i, 0)),
        pl.BlockSpec((None, rb, CONV_K - 1, CONV_WIDTH), lambda i: (l, i, 0, 0)),
        pl.BlockSpec((None, rb, LRU_CONV_K - 1, LRU_WIDTH), lambda i: (l, i, 0, 0)),
        pl.BlockSpec((None, rb, 1, LRU_WIDTH), lambda i: (l, i, 0, 0)),
    ] + [_const_spec(a.shape, l) for a in row_params] + [pl.BlockSpec(memory_space=pl.ANY)]
    out_shape = [
        jax.ShapeDtypeStruct((rows, CONV_WIDTH + LRU_WIDTH), BF16),
        jax.ShapeDtypeStruct((rows, 2 * MEM_WIDTH), F32),
        jax.ShapeDtypeStruct((depth, nbatch, CONV_K - 1, CONV_WIDTH), F32),
        jax.ShapeDtypeStruct((nbatch, LRU_CONV_K - 1, LRU_WIDTH), F32),
        jax.ShapeDtypeStruct((nbatch, 1, LRU_WIDTH), F32),
    ]
    out_specs = [
        pl.BlockSpec((rt, CONV_WIDTH + LRU_WIDTH), lambda i: (i, 0)),
        pl.BlockSpec((rt, 2 * MEM_WIDTH), lambda i: (i, 0)),
        pl.BlockSpec((None, rb, CONV_K - 1, CONV_WIDTH), lambda i: (l, i, 0, 0)),
        pl.BlockSpec((rb, LRU_CONV_K - 1, LRU_WIDTH), lambda i: (i, 0, 0)),
        pl.BlockSpec((rb, 1, LRU_WIDTH), lambda i: (i, 0, 0)),
    ]
    scratch = [
        pltpu.VMEM((rt, IN_WIDTH), F32),
        pltpu.VMEM((rt, CONV_WIDTH), F32),
        pltpu.VMEM((rt, LRU_WIDTH), F32),
        pltpu.VMEM((NCHUNK, CONV_HIST_PAD + 2 * SUBLANES, LANES), F32),
        pltpu.VMEM((NCHUNK, LRU_HIST_PAD + 2 * SUBLANES, LANES), F32),
        pltpu.VMEM((NCHUNK, CONV_K * SUBLANES, LANES), F32),
        pltpu.VMEM((NCHUNK, LRU_CONV_K * SUBLANES, LANES), F32),
        pltpu.VMEM((rt, CONV_WIDTH), F32),
        pltpu.VMEM((rt, LRU_WIDTH), F32),
        pltpu.VMEM((rt, LRU_WIDTH), F32),
        pltpu.VMEM((rt, LRU_WIDTH), F32),
    ]
    return pl.pallas_call(
        _sample_in_kernel,
        grid=grid,
        in_specs=in_specs,
        out_specs=out_specs,
        out_shape=out_shape,
        scratch_shapes=scratch,
        input_output_aliases={4 + len(row_params): 2},
        compiler_params=pltpu.CompilerParams(dimension_semantics=("arbitrary",),
                                             vmem_limit_bytes=VMEM_LIMIT),
        name=f"sample_in{l}",
    )(xs2d, cache_conv, cache_lru_conv, state_lru_h4, *row_params, nconv_all)


def _sample_out_kernel(x_ref, mixcr_ref, qg_ref, k_ref, v_ref, wout_ref, gpost_ref, y_ref, mix_ref):
    i = pl.program_id(0)
    nb = k_ref.shape[0]
    rt = qg_ref.shape[0]
    t_new = rt // nb
    per_group = SUBLANES // t_new
    r0 = pl.multiple_of(i * rt, rt)

    nq = MEM_HEADS * SUBLANES
    nkv = N_MEM * MEM_HEADS
    q_head = lax.broadcasted_iota(jnp.int32, (nq, nkv), 0) // SUBLANES
    kv_head = lax.broadcasted_iota(jnp.int32, (nq, nkv), 1) % MEM_HEADS
    valid = q_head == kv_head
    row_batch = lax.broadcasted_iota(jnp.int32, (SUBLANES, MEM_WIDTH), 0) // t_new
    scale = 1.0 / math.sqrt(MEM_HEAD_DIM)

    outs = []
    for g in range(rt // SUBLANES):
        rs = slice(g * SUBLANES, (g + 1) * SUBLANES)
        q8 = qg_ref[rs, 0:MEM_WIDTH]
        q2 = jnp.concatenate([q8[:, h * MEM_HEAD_DIM:(h + 1) * MEM_HEAD_DIM] for h in range(MEM_HEADS)],
                             axis=0).astype(BF16)
        o8 = jnp.zeros((SUBLANES, MEM_WIDTH), F32)
        for j in range(per_group):
            bi = g * per_group + j
            s = lax.dot_general(q2, k_ref[bi].astype(BF16), (((1,), (1,)), ((), ())),
                                preferred_element_type=F32)
            s = jnp.where(valid, s * scale, -1e30)
            e = jnp.exp(s - jnp.max(s, axis=-1, keepdims=True))
            l = jnp.sum(e, axis=-1, keepdims=True)
            o2 = jnp.dot(e.astype(BF16), v_ref[bi].astype(BF16), preferred_element_type=F32) / l
            o_b = jnp.concatenate([o2[h * SUBLANES:(h + 1) * SUBLANES, :] for h in range(MEM_HEADS)], axis=1)
            o8 = jnp.where(row_batch == j, o_b, o8)
        outs.append(o8 * _silu(qg_ref[rs, MEM_WIDTH:2 * MEM_WIDTH]))
    o = jnp.concatenate(outs, axis=0)
    mix_ref[pl.ds(r0, rt), 0:CONV_WIDTH + LRU_WIDTH] = mixcr_ref[...]
    mix_ref[pl.ds(r0, rt), CONV_WIDTH + LRU_WIDTH:MIX_WIDTH] = o.astype(BF16)

    @pl.when(i == pl.num_programs(0) - 1)
    def _():
        out = jnp.dot(mix_ref[...], wout_ref[...], preferred_element_type=F32)
        y_ref[...] = x_ref[...] + _rmsnorm(out, gpost_ref[...])


def _sample_out(l, xs2d, mixcr, qg, cache_k, cache_v, p):
    rows = xs2d.shape[0]
    nbatch = cache_k.shape[1]
    t_new = rows // nbatch
    ab = SAMPLE_AB
    rt = ab * t_new
    grid = (nbatch // ab,)
    in_specs = [
        pl.BlockSpec((rows, D_MODEL), lambda i: (0, 0)),
        pl.BlockSpec((rt, CONV_WIDTH + LRU_WIDTH), lambda i: (i, 0)),
        pl.BlockSpec((rt, 2 * MEM_WIDTH), lambda i: (i, 0)),
        pl.BlockSpec((None, ab, N_MEM * MEM_HEADS, MEM_HEAD_DIM), lambda i: (l, i, 0, 0)),
        pl.BlockSpec((None, ab, N_MEM * MEM_HEADS, MEM_HEAD_DIM), lambda i: (l, i, 0, 0)),
        _const_spec(p["w_out"].shape, l),
        _const_spec(p["norm_post_g"].shape, l),
    ]
    return pl.pallas_call(
        _sample_out_kernel,
        grid=grid,
        in_specs=in_specs,
        out_specs=pl.BlockSpec((rows, D_MODEL), lambda i: (0, 0)),
        out_shape=jax.ShapeDtypeStruct((rows, D_MODEL), F32),
        scratch_shapes=[pltpu.VMEM((rows, MIX_WIDTH), BF16)],
        compiler_params=pltpu.CompilerParams(dimension_semantics=("arbitrary",),
                                             vmem_limit_bytes=VMEM_LIMIT),
        name=f"sample_out{l}",
    )(xs2d, mixcr, qg, cache_k, cache_v, p["w_out"], p["norm_post_g"])


def _block_diag(w):
    h, d, _ = w.shape
    eye = jnp.eye(h, dtype=w.dtype)
    return (w[:, :, None, :] * eye[:, None, :, None]).reshape(h * d, h * d)


def kernel(x_prompt, x_sample, mem_prompt, cache_conv, cache_lru_conv, state_lru_h, cache_mem_k, cache_mem_v,
           norm_pre_g, w_in, conv_w, conv_b, conv_ln_g, conv_ln_b, lru_conv_w, lru_conv_b, lru_wa, lru_ba,
           lru_wx, lru_bx, lru_lambda, mem_norm_g, w_mem_k, w_mem_v, w_out, norm_post_g):
    depth = w_in.shape[0]
    dec_batch, dec_seq, _ = x_sample.shape

    def vec(a):
        return a.reshape(depth, 1, a.shape[-1])

    w_gate = jnp.concatenate([jax.vmap(_block_diag)(lru_wa), jax.vmap(_block_diag)(lru_wx)], axis=-1)
    params = {
        "norm_pre_g": vec(norm_pre_g), "w_in": w_in.astype(BF16),
        "conv_w": conv_w, "conv_b": vec(conv_b), "conv_ln_g": vec(conv_ln_g), "conv_ln_b": vec(conv_ln_b),
        "lru_conv_w": lru_conv_w, "lru_conv_b": vec(lru_conv_b), "w_gate": w_gate.astype(BF16),
        "lru_ba": vec(lru_ba), "lru_bx": vec(lru_bx), "lru_lambda": vec(lru_lambda),
        "w_out": w_out.astype(BF16), "norm_post_g": vec(norm_post_g),
    }

    p_mk, p_mv, p_mk_bf, p_mv_bf = _mem_kv(mem_prompt, mem_norm_g, w_mem_k.astype(BF16), w_mem_v.astype(BF16))
    s_mk = cache_mem_k.reshape(depth, dec_batch, N_MEM * MEM_HEADS, MEM_HEAD_DIM)
    s_mv = cache_mem_v.reshape(depth, dec_batch, N_MEM * MEM_HEADS, MEM_HEAD_DIM)
    s_h0 = state_lru_h.reshape(depth, dec_batch, 1, LRU_WIDTH)

    xp = x_prompt
    xs = x_sample.reshape(dec_batch * dec_seq, D_MODEL)
    p_conv, p_lconv, p_h, s_lconv, s_h = [], [], [], [], []
    s_conv = jnp.zeros(cache_conv.shape, cache_conv.dtype)
    for l in range(depth):
        xp, cb, lb, hh = _prompt_layer(l, xp, p_mk_bf, p_mv_bf, params)
        p_conv.append(cb); p_lconv.append(lb); p_h.append(hh)
        mixcr, qg, s_conv, lb2, hh2 = _sample_in(l, xs, cache_conv, cache_lru_conv, s_h0, s_conv, params)
        xs = _sample_out(l, xs, mixcr, qg, s_mk, s_mv, params)
        s_lconv.append(lb2); s_h.append(hh2.reshape(dec_batch, LRU_WIDTH))

    mem_shape = (depth, x_prompt.shape[0], N_MEM, MEM_HEADS, MEM_HEAD_DIM)
    return (xp, xs.reshape(dec_batch, dec_seq, D_MODEL),
            jnp.stack(p_conv), jnp.stack(p_lconv), jnp.stack(p_h),
            p_mk.reshape(mem_shape), p_mv.reshape(mem_shape),
            s_conv, jnp.stack(s_lconv), jnp.stack(s_h))
```

```python
import math

import jax
import jax.numpy as jnp
from jax import lax
from jax.experimental import pallas as pl
from jax.experimental.pallas import tpu as pltpu

D_MODEL = 1024
MIX_WIDTH = 2048
CONV_WIDTH = 768
LRU_WIDTH = 768
MEM_WIDTH = 512
MEM_HEADS = 4
MEM_HEAD_DIM = 128
N_MEM = 256
CONV_K = 31
LRU_CONV_K = 4
LRU_C = 8.0
EPS = 1e-6
IN_WIDTH = 4864

OFF_A, OFF_B, OFF_GC = 0, 768, 1536
OFF_XR, OFF_GR = 2304, 3072
OFF_Q, OFF_GQ = 3840, 4352

LANES = 128
SUBLANES = 8
NCHUNK = CONV_WIDTH // LANES
CONV_HIST_PAD = 32
LRU_HIST_PAD = 8
PROMPT_TM = 512
CONV_ROWS = 32
SCAN_ROWS = 16
SAMPLE_RB = 32
SAMPLE_AB = 8
VMEM_LIMIT = 56 * 1024 * 1024

BF16 = jnp.bfloat16
F32 = jnp.float32


LOG2E = 1.4426950408889634


def _sigmoid(x):
    return 1.0 / (1.0 + jnp.exp2(x * (-LOG2E)))


def _silu(x):
    return x * _sigmoid(x)


def _rmsnorm(x, g):
    return x * lax.rsqrt(jnp.mean(x * x, axis=-1, keepdims=True) + EPS) * g


def _layernorm(x, g, b):
    mu = jnp.mean(x, axis=-1, keepdims=True)
    d = x - mu
    var = jnp.mean(d * d, axis=-1, keepdims=True)
    return d * lax.rsqrt(var + EPS) * g + b


def _softplus(z):
    return jnp.maximum(z, 0.0) + jnp.log1p(jnp.exp(-jnp.abs(z)))


def _lru_coeffs(gate_a, gate_x, xc, ba, bx, c_sp):
    r = _sigmoid(gate_a + ba)
    ig = _sigmoid(gate_x + bx)
    neg_log_a = r * c_sp
    a = jnp.exp2(neg_log_a * (-LOG2E))
    y = jnp.tanh(neg_log_a) * (1.0 + a * a)
    mult = jnp.where(y > 0.0, y * lax.rsqrt(y), 0.0)
    return a, mult * (ig * xc)


def _group_scan(a, b, h_in, period):
    first = (lax.broadcasted_iota(jnp.int32, a.shape, 0) % period) == 0
    b = b + jnp.where(first, a * h_in, 0.0)
    a = jnp.where(first, 0.0, a)
    s = 1
    while s < period:
        b = a * pltpu.roll(b, s, 0) + b
        if 2 * s < period:
            a = a * pltpu.roll(a, s, 0)
        s *= 2
    return b


def _attend(q_bf, k_bf, v_bf):
    s = lax.dot_general(q_bf, k_bf, (((1,), (1,)), ((), ())), preferred_element_type=F32)
    s = s * (1.0 / math.sqrt(MEM_HEAD_DIM))
    e = jnp.exp(s - jnp.max(s, axis=-1, keepdims=True))
    l = jnp.sum(e, axis=-1, keepdims=True)
    o = jnp.dot(e.astype(BF16), v_bf, preferred_element_type=F32)
    return o / l


def _store_slabs(dst_ref, row0, val):
    for c in range(NCHUNK):
        dst_ref[c, row0:row0 + val.shape[0], :] = val[:, c * LANES:(c + 1) * LANES]


def _load_slabs(src_ref, row0, nrows):
    return jnp.concatenate([src_ref[c, row0:row0 + nrows, :] for c in range(NCHUNK)], axis=1)


def _store_taps(dst_ref, w):
    for k in range(w.shape[0]):
        for c in range(NCHUNK):
            dst_ref[c, k * SUBLANES:(k + 1) * SUBLANES, :] = jnp.broadcast_to(
                w[k:k + 1, c * LANES:(c + 1) * LANES], (SUBLANES, LANES))


def _chunk_conv(buf_ref, w_ref, c, start, taps, ngroups):
    accs = [jnp.zeros((SUBLANES, LANES), F32) for _ in range(ngroups)]
    for k in range(taps):
        w8 = w_ref[c, k * SUBLANES:(k + 1) * SUBLANES, :]
        for j in range(ngroups):
            accs[j] = accs[j] + buf_ref[c, pl.ds(start + k + j * SUBLANES, SUBLANES), :] * w8
    return accs[0] if ngroups == 1 else jnp.concatenate(accs, axis=0)


def _window_conv(buf_ref, w_ref, start, taps, ngroups):
    cols = []
    for c in range(NCHUNK):
        accs = [jnp.zeros((SUBLANES, LANES), F32) for _ in range(ngroups)]
        for k in range(taps):
            w8 = w_ref[c, k * SUBLANES:(k + 1) * SUBLANES, :]
            for j in range(ngroups):
                accs[j] = accs[j] + buf_ref[c, pl.ds(start + k + j * SUBLANES, SUBLANES), :] * w8
        cols.append(accs[0] if ngroups == 1 else jnp.concatenate(accs, axis=0))
    return jnp.concatenate(cols, axis=1)


def _mem_kv_kernel(mem_ref, g_ref, wk_ref, wv_ref, kf_ref, vf_ref, kb_ref, vb_ref):
    mn = _rmsnorm(mem_ref[...], g_ref[...]).astype(BF16)
    k = jnp.dot(mn, wk_ref[...], preferred_element_type=F32)
    v = jnp.dot(mn, wv_ref[...], preferred_element_type=F32)
    for h in range(MEM_HEADS):
        cols = slice(h * MEM_HEAD_DIM, (h + 1) * MEM_HEAD_DIM)
        kf_ref[pl.ds(h, N_MEM, stride=MEM_HEADS), :] = k[:, cols]
        vf_ref[pl.ds(h, N_MEM, stride=MEM_HEADS), :] = v[:, cols]
    kb_ref[...] = k.astype(BF16)
    vb_ref[...] = v.astype(BF16)


def _mem_kv(mem_prompt, mem_norm_g, wk_bf, wv_bf):
    depth, batch = wk_bf.shape[0], mem_prompt.shape[0]
    out_f = jax.ShapeDtypeStruct((depth, batch, N_MEM * MEM_HEADS, MEM_HEAD_DIM), F32)
    out_b = jax.ShapeDtypeStruct((depth, batch, N_MEM, MEM_WIDTH), BF16)
    flat_spec = pl.BlockSpec((None, None, N_MEM * MEM_HEADS, MEM_HEAD_DIM), lambda l, b: (l, b, 0, 0))
    return pl.pallas_call(
        _mem_kv_kernel,
        grid=(depth, batch),
        in_specs=[
            pl.BlockSpec((None, N_MEM, D_MODEL), lambda l, b: (b, 0, 0)),
            pl.BlockSpec((None, 1, D_MODEL), lambda l, b: (l, 0, 0)),
            pl.BlockSpec((None, D_MODEL, MEM_WIDTH), lambda l, b: (l, 0, 0)),
            pl.BlockSpec((None, D_MODEL, MEM_WIDTH), lambda l, b: (l, 0, 0)),
        ],
        out_specs=[
            flat_spec, flat_spec,
            pl.BlockSpec((None, None, N_MEM, MEM_WIDTH), lambda l, b: (l, b, 0, 0)),
            pl.BlockSpec((None, None, N_MEM, MEM_WIDTH), lambda l, b: (l, b, 0, 0)),
        ],
        out_shape=[out_f, out_f, out_b, out_b],
        compiler_params=pltpu.CompilerParams(dimension_semantics=("arbitrary", "arbitrary")),
        name="mem_kv",
    )(mem_prompt, mem_norm_g.reshape(depth, 1, D_MODEL), wk_bf, wv_bf)


def _prompt_layer_kernel(x_ref, k_ref, v_ref, gpre_ref, win_ref, cw_ref, cb_ref, lng_ref, lnb_ref,
                         lcw_ref, lcb_ref, wg_ref, ba_ref, bx_ref, lam_ref, wout_ref, gpost_ref,
                         y_ref, nconv_ref, nlru_ref, nh_ref,
                         proj_ref, gate_ref, conv_ref, ubuf_ref, xrbuf_ref, cws_ref, lcws_ref, xc_ref, mix_ref,
                         hcarry_ref):
    b = pl.program_id(0)
    t = pl.program_id(1)
    tm = x_ref.shape[0]

    @pl.when(t == 0)
    def _():
        for c in range(NCHUNK):
            ubuf_ref[c, 0:CONV_HIST_PAD, :] = jnp.zeros((CONV_HIST_PAD, LANES), F32)
            xrbuf_ref[c, 0:LRU_HIST_PAD, :] = jnp.zeros((LRU_HIST_PAD, LANES), F32)
        hcarry_ref[...] = jnp.zeros(hcarry_ref.shape, F32)
        _store_taps(cws_ref, cw_ref[...])
        _store_taps(lcws_ref, lcw_ref[...])

    xn = _rmsnorm(x_ref[...], gpre_ref[...]).astype(BF16)

    def in_proj(c0, c1):
        proj_ref[:, c0:c1] = jnp.dot(xn, win_ref[:, c0:c1], preferred_element_type=F32)

    def conv_lane_chunk(c):
        in_proj(2 * c * LANES, 2 * (c + 1) * LANES)
        glu = (proj_ref[:, 2 * c * LANES:(2 * c + 1) * LANES]
               * _sigmoid(proj_ref[:, (2 * c + 1) * LANES:(2 * c + 2) * LANES]))
        ubuf_ref[c, CONV_HIST_PAD:CONV_HIST_PAD + tm, :] = glu
        for i in range(tm // CONV_ROWS):
            r0 = i * CONV_ROWS
            conv_ref[r0:r0 + CONV_ROWS, c * LANES:(c + 1) * LANES] = _chunk_conv(
                ubuf_ref, cws_ref, c, r0 + (CONV_HIST_PAD - (CONV_K - 1)), CONV_K, CONV_ROWS // SUBLANES)

    in_proj(OFF_XR, OFF_GR)
    conv_lane_chunk(0)
    _store_slabs(xrbuf_ref, LRU_HIST_PAD, proj_ref[:, OFF_XR:OFF_XR + LRU_WIDTH])
    for i in range(tm // CONV_ROWS):
        r0 = i * CONV_ROWS
        xc = _window_conv(xrbuf_ref, lcws_ref, r0 + (LRU_HIST_PAD - (LRU_CONV_K - 1)), LRU_CONV_K,
                          CONV_ROWS // SUBLANES)
        xc_ref[r0:r0 + CONV_ROWS, :] = xc + lcb_ref[...]
    conv_lane_chunk(1)
    gate_ref[...] = jnp.dot(xc_ref[...].astype(BF16), wg_ref[...], preferred_element_type=F32)
    for c in range(2, NCHUNK):
        conv_lane_chunk(c)
    in_proj(OFF_GC, OFF_XR)
    in_proj(OFF_GR, IN_WIDTH)

    c_sp = LRU_C * _softplus(-lam_ref[...])
    ba = ba_ref[...]
    bx = bx_ref[...]

    def scan_chunk(i, hprev):
        r0 = i * SCAN_ROWS
        hs = []
        for j in range(SCAN_ROWS // SUBLANES):
            rows = slice(r0 + j * SUBLANES, r0 + (j + 1) * SUBLANES)
            a, bt = _lru_coeffs(gate_ref[rows, 0:LRU_WIDTH], gate_ref[rows, LRU_WIDTH:2 * LRU_WIDTH],
                                xc_ref[rows, :], ba, bx, c_sp)
            h = _group_scan(a, bt, hprev, SUBLANES)
            hprev = jnp.broadcast_to(h[SUBLANES - 1:SUBLANES, :], (SUBLANES, LRU_WIDTH))
            hs.append(h)
        h16 = jnp.concatenate(hs, axis=0)
        rr = h16 * _silu(proj_ref[r0:r0 + SCAN_ROWS, OFF_GR:OFF_GR + LRU_WIDTH])
        mix_ref[r0:r0 + SCAN_ROWS, CONV_WIDTH:CONV_WIDTH + LRU_WIDTH] = rr.astype(BF16)
        return hprev

    def norm_chunk(i):
        r0 = i * CONV_ROWS
        c = _layernorm(conv_ref[r0:r0 + CONV_ROWS, :] + cb_ref[...], lng_ref[...], lnb_ref[...])
        c = _silu(c) * _silu(proj_ref[r0:r0 + CONV_ROWS, OFF_GC:OFF_GC + CONV_WIDTH])
        mix_ref[r0:r0 + CONV_ROWS, 0:CONV_WIDTH] = c.astype(BF16)

    n_scan = tm // SCAN_ROWS
    n_conv = tm // CONV_ROWS
    hprev = hcarry_ref[...]
    for i in range(n_scan):
        hprev = scan_chunk(i, hprev)
        if (i + 1) % (n_scan // n_conv) == 0:
            norm_chunk((i + 1) // (n_scan // n_conv) - 1)
    hlast = hprev
    hcarry_ref[...] = hlast

    for c in range(NCHUNK):
        ubuf_ref[c, 0:CONV_HIST_PAD, :] = ubuf_ref[c, tm:tm + CONV_HIST_PAD, :]
        xrbuf_ref[c, 0:LRU_HIST_PAD, :] = xrbuf_ref[c, tm:tm + LRU_HIST_PAD, :]

    for h in range(MEM_HEADS):
        cols = slice(h * MEM_HEAD_DIM, (h + 1) * MEM_HEAD_DIM)
        q = proj_ref[:, OFF_Q + h * MEM_HEAD_DIM:OFF_Q + (h + 1) * MEM_HEAD_DIM].astype(BF16)
        o = _attend(q, k_ref[:, cols], v_ref[:, cols])
        o = o * _silu(proj_ref[:, OFF_GQ + h * MEM_HEAD_DIM:OFF_GQ + (h + 1) * MEM_HEAD_DIM])
        mix_ref[:, CONV_WIDTH + LRU_WIDTH + h * MEM_HEAD_DIM:
                CONV_WIDTH + LRU_WIDTH + (h + 1) * MEM_HEAD_DIM] = o.astype(BF16)

    def out_proj(c0, c1):
        return jnp.dot(mix_ref[:, c0:c1], wout_ref[c0:c1, :], preferred_element_type=F32)

    out = (out_proj(0, CONV_WIDTH) + out_proj(CONV_WIDTH + LRU_WIDTH, MIX_WIDTH)
           + out_proj(CONV_WIDTH, CONV_WIDTH + LRU_WIDTH))
    y_ref[...] = x_ref[...] + _rmsnorm(out, gpost_ref[...])

    @pl.when(t == pl.num_programs(1) - 1)
    def _():
        nconv_ref[...] = _load_slabs(ubuf_ref, CONV_HIST_PAD - (CONV_K - 1), CONV_K - 1)
        nlru_ref[...] = _load_slabs(xrbuf_ref, LRU_HIST_PAD - (LRU_CONV_K - 1), LRU_CONV_K - 1)
        nh_ref[pl.ds(b, 1), :] = hlast[0:1, :]


def _const_spec(shape, l):
    nd = len(shape)
    return pl.BlockSpec((None,) + tuple(shape[1:]), lambda *_: (l,) + (0,) * (nd - 1))


def _prompt_layer(l, x, mk, mv, p):
    batch, seq, _ = x.shape
    tm = PROMPT_TM
    grid = (batch, seq // tm)
    row_params = [p["norm_pre_g"], p["w_in"], p["conv_w"], p["conv_b"], p["conv_ln_g"], p["conv_ln_b"],
                  p["lru_conv_w"], p["lru_conv_b"], p["w_gate"], p["lru_ba"], p["lru_bx"], p["lru_lambda"],
                  p["w_out"], p["norm_post_g"]]
    in_specs = [
        pl.BlockSpec((None, tm, D_MODEL), lambda b, t: (b, t, 0)),
        pl.BlockSpec((None, None, N_MEM, MEM_WIDTH), lambda b, t: (l, b, 0, 0)),
        pl.BlockSpec((None, None, N_MEM, MEM_WIDTH), lambda b, t: (l, b, 0, 0)),
    ] + [_const_spec(a.shape, l) for a in row_params]
    out_shape = [
        jax.ShapeDtypeStruct((batch, seq, D_MODEL), F32),
        jax.ShapeDtypeStruct((batch, CONV_K - 1, CONV_WIDTH), F32),
        jax.ShapeDtypeStruct((batch, LRU_CONV_K - 1, LRU_WIDTH), F32),
        jax.ShapeDtypeStruct((batch, LRU_WIDTH), F32),
    ]
    out_specs = [
        pl.BlockSpec((None, tm, D_MODEL), lambda b, t: (b, t, 0)),
        pl.BlockSpec((None, CONV_K - 1, CONV_WIDTH), lambda b, t: (b, 0, 0)),
        pl.BlockSpec((None, LRU_CONV_K - 1, LRU_WIDTH), lambda b, t: (b, 0, 0)),
        pl.BlockSpec((batch, LRU_WIDTH), lambda b, t: (0, 0)),
    ]
    scratch = [
        pltpu.VMEM((tm, IN_WIDTH), F32),
        pltpu.VMEM((tm, 2 * LRU_WIDTH), F32),
        pltpu.VMEM((tm, CONV_WIDTH), F32),
        pltpu.VMEM((NCHUNK, CONV_HIST_PAD + tm, LANES), F32),
        pltpu.VMEM((NCHUNK, LRU_HIST_PAD + tm, LANES), F32),
        pltpu.VMEM((NCHUNK, CONV_K * SUBLANES, LANES), F32),
        pltpu.VMEM((NCHUNK, LRU_CONV_K * SUBLANES, LANES), F32),
        pltpu.VMEM((tm, LRU_WIDTH), F32),
        pltpu.VMEM((tm, MIX_WIDTH), BF16),
        pltpu.VMEM((SUBLANES, LRU_WIDTH), F32),
    ]
    return pl.pallas_call(
        _prompt_layer_kernel,
        grid=grid,
        in_specs=in_specs,
        out_specs=out_specs,
        out_shape=out_shape,
        scratch_shapes=scratch,
        compiler_params=pltpu.CompilerParams(dimension_semantics=("arbitrary", "arbitrary"),
                                             vmem_limit_bytes=VMEM_LIMIT),
        name=f"prompt_layer{l}",
    )(x, mk, mv, *row_params)


def _sample_in_kernel(x_ref, cconv_ref, clru_ref, h0_ref, gpre_ref, win_ref, cw_ref, cb_ref, lng_ref, lnb_ref,
                      lcw_ref, lcb_ref, wg_ref, ba_ref, bx_ref, lam_ref, nconv_all_ref,
                      mixcr_ref, qg_ref, nconv_ref, nlru_ref, nh_ref,
                      proj_ref, u_ref, xr_ref, cwin_ref, lwin_ref, cws_ref, lcws_ref,
                      conv_ref, xc_ref, h0rep_ref, h_ref):
    rows = x_ref.shape[0]
    nb = cconv_ref.shape[0]
    t_new = rows // nb
    per_group = SUBLANES // t_new
    hist_c = CONV_K - 1
    hist_l = LRU_CONV_K - 1
    row8 = lax.broadcasted_iota(jnp.int32, (SUBLANES, CONV_WIDTH), 0)

    @pl.when(pl.program_id(0) == 0)
    def _():
        _store_taps(cws_ref, cw_ref[...])
        _store_taps(lcws_ref, lcw_ref[...])

    xn = _rmsnorm(x_ref[...], gpre_ref[...]).astype(BF16)
    proj_ref[...] = jnp.dot(xn, win_ref[...], preferred_element_type=F32)
    for c in range(NCHUNK):
        u_ref[:, c * LANES:(c + 1) * LANES] = (proj_ref[:, 2 * c * LANES:(2 * c + 1) * LANES]
                                               * _sigmoid(proj_ref[:, (2 * c + 1) * LANES:(2 * c + 2) * LANES]))
    xr_ref[...] = proj_ref[:, OFF_XR:OFF_XR + LRU_WIDTH]
    qg_ref[...] = proj_ref[:, OFF_Q:OFF_Q + 2 * MEM_WIDTH]

    c_new0 = CONV_HIST_PAD
    l_new0 = LRU_HIST_PAD

    def per_group_body(g, carry):
        r0 = pl.multiple_of(g * SUBLANES, SUBLANES)
        _store_slabs(cwin_ref, c_new0, u_ref[pl.ds(r0, SUBLANES), :])
        _store_slabs(lwin_ref, l_new0, xr_ref[pl.ds(r0, SUBLANES), :])
        conv8 = jnp.zeros((SUBLANES, CONV_WIDTH), F32)
        xc8 = jnp.zeros((SUBLANES, LRU_WIDTH), F32)
        h08 = jnp.zeros((SUBLANES, LRU_WIDTH), F32)
        for j in range(per_group):
            bi = g * per_group + j
            mine = (row8 // t_new) == j
            c0 = c_new0 + j * t_new - hist_c
            _store_slabs(cwin_ref, c0, cconv_ref[bi])
            nconv_ref[bi] = _load_slabs(cwin_ref, c0 + t_new, hist_c)
            acc = _window_conv(cwin_ref, cws_ref, c0, CONV_K, 1)
            conv8 = jnp.where(mine, pltpu.roll(acc, j * t_new, 0) if j else acc, conv8)
            l0 = l_new0 + j * t_new - hist_l
            _store_slabs(lwin_ref, l0, clru_ref[bi])
            nlru_ref[bi] = _load_slabs(lwin_ref, l0 + t_new, hist_l)
            xc = _window_conv(lwin_ref, lcws_ref, l0, LRU_CONV_K, 1)
            xc8 = jnp.where(mine, pltpu.roll(xc, j * t_new, 0) if j else xc, xc8)
            h08 = jnp.where(mine, jnp.broadcast_to(h0_ref[bi], (SUBLANES, LRU_WIDTH)), h08)
        conv_ref[pl.ds(r0, SUBLANES), :] = conv8 + cb_ref[...]
        xc_ref[pl.ds(r0, SUBLANES), :] = xc8 + lcb_ref[...]
        h0rep_ref[pl.ds(r0, SUBLANES), :] = h08
        return carry

    lax.fori_loop(0, rows // SUBLANES, per_group_body, 0)

    c = _layernorm(conv_ref[...], lng_ref[...], lnb_ref[...])
    c = _silu(c) * _silu(proj_ref[:, OFF_GC:OFF_GC + CONV_WIDTH])
    mixcr_ref[:, 0:CONV_WIDTH] = c.astype(BF16)

    proj_ref[:, 0:2 * LRU_WIDTH] = jnp.dot(xc_ref[...].astype(BF16), wg_ref[...], preferred_element_type=F32)
    c_sp = LRU_C * _softplus(-lam_ref[...])
    ba = ba_ref[...]
    bx = bx_ref[...]

    def scan_group(g, carry):
        rs = pl.ds(pl.multiple_of(g * SUBLANES, SUBLANES), SUBLANES)
        a, bt = _lru_coeffs(proj_ref[rs, 0:LRU_WIDTH], proj_ref[rs, LRU_WIDTH:2 * LRU_WIDTH],
                            xc_ref[rs, :], ba, bx, c_sp)
        h = _group_scan(a, bt, h0rep_ref[rs, :], t_new)
        h_ref[rs, :] = h * _silu(proj_ref[rs, OFF_GR:OFF_GR + LRU_WIDTH])
        for j in range(per_group):
            nh_ref[g * per_group + j] = h[(j + 1) * t_new - 1:(j + 1) * t_new, :]
        return carry

    lax.fori_loop(0, rows // SUBLANES, scan_group, 0)
    mixcr_ref[:, CONV_WIDTH:CONV_WIDTH + LRU_WIDTH] = h_ref[...].astype(BF16)


def _sample_in(l, xs2d, cache_conv, cache_lru_conv, state_lru_h4, nconv_all, p):
    rows = xs2d.shape[0]
    depth, nbatch = cache_conv.shape[0], cache_conv.shape[1]
    t_new = rows // nbatch
    assert SUBLANES % t_new == 0
    rb = SAMPLE_RB
    rt = rb * t_new
    grid = (nbatch // rb,)
    row_params = [p["norm_pre_g"], p["w_in"], p["conv_w"], p["conv_b"], p["conv_ln_g"], p["conv_ln_b"],
                  p["lru_conv_w"], p["lru_conv_b"], p["w_gate"], p["lru_ba"], p["lru_bx"], p["lru_lambda"]]
    in_specs = [
        pl.BlockSpec((rt, D_MODEL), lambda i: (i, 0)),
        pl.BlockSpec((None, rb, CONV_K - 1, CONV_WIDTH), lambda i: (l, i, 0, 0)),
        pl.BlockSpec((None, rb, LRU_CONV_K - 1, LRU_WIDTH), lambda i: (l, i, 0, 0)),
        pl.BlockSpec((None, rb, 1, LRU_WIDTH), lambda i: (l, i, 0, 0)),
    ] + [_const_spec(a.shape, l) for a in row_params] + [pl.BlockSpec(memory_space=pl.ANY)]
    out_shape = [
        jax.ShapeDtypeStruct((rows, CONV_WIDTH + LRU_WIDTH), BF16),
        jax.ShapeDtypeStruct((rows, 2 * MEM_WIDTH), F32),
        jax.ShapeDtypeStruct((depth, nbatch, CONV_K - 1, CONV_WIDTH), F32),
        jax.ShapeDtypeStruct((nbatch, LRU_CONV_K - 1, LRU_WIDTH), F32),
        jax.ShapeDtypeStruct((nbatch, 1, LRU_WIDTH), F32),
    ]
    out_specs = [
        pl.BlockSpec((rt, CONV_WIDTH + LRU_WIDTH), lambda i: (i, 0)),
        pl.BlockSpec((rt, 2 * MEM_WIDTH), lambda i: (i, 0)),
        pl.BlockSpec((None, rb, CONV_K - 1, CONV_WIDTH), lambda i: (l, i, 0, 0)),
        pl.BlockSpec((rb, LRU_CONV_K - 1, LRU_WIDTH), lambda i: (i, 0, 0)),
        pl.BlockSpec((rb, 1, LRU_WIDTH), lambda i: (i, 0, 0)),
    ]
    scratch = [
        pltpu.VMEM((rt, IN_WIDTH), F32),
        pltpu.VMEM((rt, CONV_WIDTH), F32),
        pltpu.VMEM((rt, LRU_WIDTH), F32),
        pltpu.VMEM((NCHUNK, CONV_HIST_PAD + 2 * SUBLANES, LANES), F32),
        pltpu.VMEM((NCHUNK, LRU_HIST_PAD + 2 * SUBLANES, LANES), F32),
        pltpu.VMEM((NCHUNK, CONV_K * SUBLANES, LANES), F32),
        pltpu.VMEM((NCHUNK, LRU_CONV_K * SUBLANES, LANES), F32),
        pltpu.VMEM((rt, CONV_WIDTH), F32),
        pltpu.VMEM((rt, LRU_WIDTH), F32),
        pltpu.VMEM((rt, LRU_WIDTH), F32),
        pltpu.VMEM((rt, LRU_WIDTH), F32),
    ]
    return pl.pallas_call(
        _sample_in_kernel,
        grid=grid,
        in_specs=in_specs,
        out_specs=out_specs,
        out_shape=out_shape,
        scratch_shapes=scratch,
        input_output_aliases={4 + len(row_params): 2},
        compiler_params=pltpu.CompilerParams(dimension_semantics=("arbitrary",),
                                             vmem_limit_bytes=VMEM_LIMIT),
        name=f"sample_in{l}",
    )(xs2d, cache_conv, cache_lru_conv, state_lru_h4, *row_params, nconv_all)


def _sample_out_kernel(x_ref, mixcr_ref, qg_ref, k_ref, v_ref, wout_ref, gpost_ref, y_ref, mix_ref):
    i = pl.program_id(0)
    nb = k_ref.shape[0]
    rt = qg_ref.shape[0]
    t_new = rt // nb
    per_group = SUBLANES // t_new
    r0 = pl.multiple_of(i * rt, rt)

    nq = MEM_HEADS * SUBLANES
    nkv = N_MEM * MEM_HEADS
    q_head = lax.broadcasted_iota(jnp.int32, (nq, nkv), 0) // SUBLANES
    kv_head = lax.broadcasted_iota(jnp.int32, (nq, nkv), 1) % MEM_HEADS
    valid = q_head == kv_head
    row_batch = lax.broadcasted_iota(jnp.int32, (SUBLANES, MEM_WIDTH), 0) // t_new
    scale = 1.0 / math.sqrt(MEM_HEAD_DIM)

    outs = []
    for g in range(rt // SUBLANES):
        rs = slice(g * SUBLANES, (g + 1) * SUBLANES)
        q8 = qg_ref[rs, 0:MEM_WIDTH]
        q2 = jnp.concatenate([q8[:, h * MEM_HEAD_DIM:(h + 1) * MEM_HEAD_DIM] for h in range(MEM_HEADS)],
                             axis=0).astype(BF16)
        o8 = jnp.zeros((SUBLANES, MEM_WIDTH), F32)
        for j in range(per_group):
            bi = g * per_group + j
            s = lax.dot_general(q2, k_ref[bi].astype(BF16), (((1,), (1,)), ((), ())),
                                preferred_element_type=F32)
            s = jnp.where(valid, s * scale, -1e30)
            e = jnp.exp(s - jnp.max(s, axis=-1, keepdims=True))
            l = jnp.sum(e, axis=-1, keepdims=True)
            o2 = jnp.dot(e.astype(BF16), v_ref[bi].astype(BF16), preferred_element_type=F32) / l
            o_b = jnp.concatenate([o2[h * SUBLANES:(h + 1) * SUBLANES, :] for h in range(MEM_HEADS)], axis=1)
            o8 = jnp.where(row_batch == j, o_b, o8)
        outs.append(o8 * _silu(qg_ref[rs, MEM_WIDTH:2 * MEM_WIDTH]))
    o = jnp.concatenate(outs, axis=0)
    mix_ref[pl.ds(r0, rt), 0:CONV_WIDTH + LRU_WIDTH] = mixcr_ref[...]
    mix_ref[pl.ds(r0, rt), CONV_WIDTH + LRU_WIDTH:MIX_WIDTH] = o.astype(BF16)

    @pl.when(i == pl.num_programs(0) - 1)
    def _():
        out = jnp.dot(mix_ref[...], wout_ref[...], preferred_element_type=F32)
        y_ref[...] = x_ref[...] + _rmsnorm(out, gpost_ref[...])


def _sample_out(l, xs2d, mixcr, qg, cache_k, cache_v, p):
    rows = xs2d.shape[0]
    nbatch = cache_k.shape[1]
    t_new = rows // nbatch
    ab = SAMPLE_AB
    rt = ab * t_new
    grid = (nbatch // ab,)
    in_specs = [
        pl.BlockSpec((rows, D_MODEL), lambda i: (0, 0)),
        pl.BlockSpec((rt, CONV_WIDTH + LRU_WIDTH), lambda i: (i, 0)),
        pl.BlockSpec((rt, 2 * MEM_WIDTH), lambda i: (i, 0)),
        pl.BlockSpec((None, ab, N_MEM * MEM_HEADS, MEM_HEAD_DIM), lambda i: (l, i, 0, 0)),
        pl.BlockSpec((None, ab, N_MEM * MEM_HEADS, MEM_HEAD_DIM), lambda i: (l, i, 0, 0)),
        _const_spec(p["w_out"].shape, l),
        _const_spec(p["norm_post_g"].shape, l),
    ]
    return pl.pallas_call(
        _sample_out_kernel,
        grid=grid,
        in_specs=in_specs,
        out_specs=pl.BlockSpec((rows, D_MODEL), lambda i: (0, 0)),
        out_shape=jax.ShapeDtypeStruct((rows, D_MODEL), F32),
        scratch_shapes=[pltpu.VMEM((rows, MIX_WIDTH), BF16)],
        compiler_params=pltpu.CompilerParams(dimension_semantics=("arbitrary",),
                                             vmem_limit_bytes=VMEM_LIMIT),
        name=f"sample_out{l}",
    )(xs2d, mixcr, qg, cache_k, cache_v, p["w_out"], p["norm_post_g"])


def _block_diag(w):
    h, d, _ = w.shape
    eye = jnp.eye(h, dtype=w.dtype)
    return (w[:, :, None, :] * eye[:, None, :, None]).reshape(h * d, h * d)


def kernel(x_prompt, x_sample, mem_prompt, cache_conv, cache_lru_conv, state_lru_h, cache_mem_k, cache_mem_v,
           norm_pre_g, w_in, conv_w, conv_b, conv_ln_g, conv_ln_b, lru_conv_w, lru_conv_b, lru_wa, lru_ba,
           lru_wx, lru_bx, lru_lambda, mem_norm_g, w_mem_k, w_mem_v, w_out, norm_post_g):
    depth = w_in.shape[0]
    dec_batch, dec_seq, _ = x_sample.shape

    def vec(a):
        return a.reshape(depth, 1, a.shape[-1])

    w_gate = jnp.concatenate([jax.vmap(_block_diag)(lru_wa), jax.vmap(_block_diag)(lru_wx)], axis=-1)
    w_glu = jnp.stack([w_in[:, :, OFF_A:OFF_A + CONV_WIDTH].reshape(depth, D_MODEL, NCHUNK, LANES),
                       w_in[:, :, OFF_B:OFF_B + CONV_WIDTH].reshape(depth, D_MODEL, NCHUNK, LANES)], axis=3)
    w_in_glu = jnp.concatenate([w_glu.reshape(depth, D_MODEL, 2 * CONV_WIDTH), w_in[:, :, OFF_GC:]], axis=-1)
    params = {
        "norm_pre_g": vec(norm_pre_g), "w_in": w_in_glu.astype(BF16),
        "conv_w": conv_w, "conv_b": vec(conv_b), "conv_ln_g": vec(conv_ln_g), "conv_ln_b": vec(conv_ln_b),
        "lru_conv_w": lru_conv_w, "lru_conv_b": vec(lru_conv_b), "w_gate": w_gate.astype(BF16),
        "lru_ba": vec(lru_ba), "lru_bx": vec(lru_bx), "lru_lambda": vec(lru_lambda),
        "w_out": w_out.astype(BF16), "norm_post_g": vec(norm_post_g),
    }

    p_mk, p_mv, p_mk_bf, p_mv_bf = _mem_kv(mem_prompt, mem_norm_g, w_mem_k.astype(BF16), w_mem_v.astype(BF16))
    s_mk = cache_mem_k.reshape(depth, dec_batch, N_MEM * MEM_HEADS, MEM_HEAD_DIM)
    s_mv = cache_mem_v.reshape(depth, dec_batch, N_MEM * MEM_HEADS, MEM_HEAD_DIM)
    s_h0 = state_lru_h.reshape(depth, dec_batch, 1, LRU_WIDTH)

    xp = x_prompt
    xs = x_sample.reshape(dec_batch * dec_seq, D_MODEL)
    p_conv, p_lconv, p_h, s_lconv, s_h = [], [], [], [], []
    s_conv = jnp.zeros(cache_conv.shape, cache_conv.dtype)
    for l in range(depth):
        xp, cb, lb, hh = _prompt_layer(l, xp, p_mk_bf, p_mv_bf, params)
        p_conv.append(cb); p_lconv.append(lb); p_h.append(hh)
        mixcr, qg, s_conv, lb2, hh2 = _sample_in(l, xs, cache_conv, cache_lru_conv, s_h0, s_conv, params)
        xs = _sample_out(l, xs, mixcr, qg, s_mk, s_mv, params)
        s_lconv.append(lb2); s_h.append(hh2.reshape(dec_batch, LRU_WIDTH))

    mem_shape = (depth, x_prompt.shape[0], N_MEM, MEM_HEADS, MEM_HEAD_DIM)
    return (xp, xs.reshape(dec_batch, dec_seq, D_MODEL),
            jnp.stack(p_conv), jnp.stack(p_lconv), jnp.stack(p_h),
            p_mk.reshape(mem_shape), p_mv.reshape(mem_shape),
            s_conv, jnp.stack(s_lconv), jnp.stack(s_h))
```

```python
import math

import jax
import jax.numpy as jnp
from jax import lax
from jax.experimental import pallas as pl
from jax.experimental.pallas import tpu as pltpu

D_MODEL = 1024
MIX_WIDTH = 2048
CONV_WIDTH = 768
LRU_WIDTH = 768
MEM_WIDTH = 512
MEM_HEADS = 4
MEM_HEAD_DIM = 128
N_MEM = 256
CONV_K = 31
LRU_CONV_K = 4
LRU_C = 8.0
EPS = 1e-6
IN_WIDTH = 4864

OFF_A, OFF_B, OFF_GC = 0, 768, 1536
OFF_XR, OFF_GR = 2304, 3072
OFF_Q, OFF_GQ = 3840, 4352

LANES = 128
SUBLANES = 8
NCHUNK = CONV_WIDTH // LANES
CONV_HIST_PAD = 32
LRU_HIST_PAD = 8
PROMPT_TM = 512
CONV_ROWS = 32
SCAN_ROWS = 16
SAMPLE_RB = 32
SAMPLE_AB = 8
VMEM_LIMIT = 56 * 1024 * 1024

BF16 = jnp.bfloat16
F32 = jnp.float32


LOG2E = 1.4426950408889634


def _sigmoid(x):
    return 1.0 / (1.0 + jnp.exp2(x * (-LOG2E)))


def _silu(x):
    return x * _sigmoid(x)


def _rmsnorm(x, g):
    return x * lax.rsqrt(jnp.mean(x * x, axis=-1, keepdims=True) + EPS) * g


def _layernorm(x, g, b):
    mu = jnp.mean(x, axis=-1, keepdims=True)
    d = x - mu
    var = jnp.mean(d * d, axis=-1, keepdims=True)
    return d * lax.rsqrt(var + EPS) * g + b


def _softplus(z):
    return jnp.maximum(z, 0.0) + jnp.log1p(jnp.exp(-jnp.abs(z)))


def _lru_coeffs(gate_a, gate_x, xc, ba, bx, c_sp):
    r = _sigmoid(gate_a + ba)
    ig = _sigmoid(gate_x + bx)
    neg_log_a = r * c_sp
    a = jnp.exp2(neg_log_a * (-LOG2E))
    y = jnp.tanh(neg_log_a) * (1.0 + a * a)
    mult = jnp.where(y > 0.0, y * lax.rsqrt(y), 0.0)
    return a, mult * (ig * xc)


def _group_scan(a, b, h_in, period):
    first = (lax.broadcasted_iota(jnp.int32, a.shape, 0) % period) == 0
    b = b + jnp.where(first, a * h_in, 0.0)
    a = jnp.where(first, 0.0, a)
    s = 1
    while s < period:
        b = a * pltpu.roll(b, s, 0) + b
        if 2 * s < period:
            a = a * pltpu.roll(a, s, 0)
        s *= 2
    return b


def _attend(q_bf, k_bf, v_bf):
    s = lax.dot_general(q_bf, k_bf, (((1,), (1,)), ((), ())), preferred_element_type=F32)
    s = s * (1.0 / math.sqrt(MEM_HEAD_DIM))
    e = jnp.exp(s - jnp.max(s, axis=-1, keepdims=True))
    l = jnp.sum(e, axis=-1, keepdims=True)
    o = jnp.dot(e.astype(BF16), v_bf, preferred_element_type=F32)
    return o / l


def _store_slabs(dst_ref, row0, val):
    for c in range(NCHUNK):
        dst_ref[c, row0:row0 + val.shape[0], :] = val[:, c * LANES:(c + 1) * LANES]


def _load_slabs(src_ref, row0, nrows):
    return jnp.concatenate([src_ref[c, row0:row0 + nrows, :] for c in range(NCHUNK)], axis=1)


def _store_taps(dst_ref, w):
    for k in range(w.shape[0]):
        for c in range(NCHUNK):
            dst_ref[c, k * SUBLANES:(k + 1) * SUBLANES, :] = jnp.broadcast_to(
                w[k:k + 1, c * LANES:(c + 1) * LANES], (SUBLANES, LANES))


def _chunk_conv(buf_ref, w_ref, c, start, taps, ngroups):
    accs = [jnp.zeros((SUBLANES, LANES), F32) for _ in range(ngroups)]
    for k in range(taps):
        w8 = w_ref[c, k * SUBLANES:(k + 1) * SUBLANES, :]
        for j in range(ngroups):
            accs[j] = accs[j] + buf_ref[c, pl.ds(start + k + j * SUBLANES, SUBLANES), :] * w8
    return accs[0] if ngroups == 1 else jnp.concatenate(accs, axis=0)


def _window_conv(buf_ref, w_ref, start, taps, ngroups):
    cols = []
    for c in range(NCHUNK):
        accs = [jnp.zeros((SUBLANES, LANES), F32) for _ in range(ngroups)]
        for k in range(taps):
            w8 = w_ref[c, k * SUBLANES:(k + 1) * SUBLANES, :]
            for j in range(ngroups):
                accs[j] = accs[j] + buf_ref[c, pl.ds(start + k + j * SUBLANES, SUBLANES), :] * w8
        cols.append(accs[0] if ngroups == 1 else jnp.concatenate(accs, axis=0))
    return jnp.concatenate(cols, axis=1)


def _mem_kv_kernel(mem_ref, g_ref, wk_ref, wv_ref, kf_ref, vf_ref, kb_ref, vb_ref):
    mn = _rmsnorm(mem_ref[...], g_ref[...]).astype(BF16)
    k = jnp.dot(mn, wk_ref[...], preferred_element_type=F32)
    v = jnp.dot(mn, wv_ref[...], preferred_element_type=F32)
    for h in range(MEM_HEADS):
        cols = slice(h * MEM_HEAD_DIM, (h + 1) * MEM_HEAD_DIM)
        kf_ref[pl.ds(h, N_MEM, stride=MEM_HEADS), :] = k[:, cols]
        vf_ref[pl.ds(h, N_MEM, stride=MEM_HEADS), :] = v[:, cols]
    kb_ref[...] = k.astype(BF16)
    vb_ref[...] = v.astype(BF16)


def _mem_kv(mem_prompt, mem_norm_g, wk_bf, wv_bf):
    depth, batch = wk_bf.shape[0], mem_prompt.shape[0]
    out_f = jax.ShapeDtypeStruct((depth, batch, N_MEM * MEM_HEADS, MEM_HEAD_DIM), F32)
    out_b = jax.ShapeDtypeStruct((depth, batch, N_MEM, MEM_WIDTH), BF16)
    flat_spec = pl.BlockSpec((None, None, N_MEM * MEM_HEADS, MEM_HEAD_DIM), lambda l, b: (l, b, 0, 0))
    return pl.pallas_call(
        _mem_kv_kernel,
        grid=(depth, batch),
        in_specs=[
            pl.BlockSpec((None, N_MEM, D_MODEL), lambda l, b: (b, 0, 0)),
            pl.BlockSpec((None, 1, D_MODEL), lambda l, b: (l, 0, 0)),
            pl.BlockSpec((None, D_MODEL, MEM_WIDTH), lambda l, b: (l, 0, 0)),
            pl.BlockSpec((None, D_MODEL, MEM_WIDTH), lambda l, b: (l, 0, 0)),
        ],
        out_specs=[
            flat_spec, flat_spec,
            pl.BlockSpec((None, None, N_MEM, MEM_WIDTH), lambda l, b: (l, b, 0, 0)),
            pl.BlockSpec((None, None, N_MEM, MEM_WIDTH), lambda l, b: (l, b, 0, 0)),
        ],
        out_shape=[out_f, out_f, out_b, out_b],
        compiler_params=pltpu.CompilerParams(dimension_semantics=("arbitrary", "arbitrary")),
        name="mem_kv",
    )(mem_prompt, mem_norm_g.reshape(depth, 1, D_MODEL), wk_bf, wv_bf)


def _prompt_layer_kernel(x_ref, k_ref, v_ref, gpre_ref, win_ref, cw_ref, cb_ref, lng_ref, lnb_ref,
                         lcw_ref, lcb_ref, wg_ref, ba_ref, bx_ref, lam_ref, wout_ref, gpost_ref,
                         y_ref, nconv_ref, nlru_ref, nh_ref,
                         proj_ref, gate_ref, conv_ref, ubuf_ref, xrbuf_ref, cws_ref, lcws_ref, xc_ref, mix_ref,
                         hcarry_ref):
    b = pl.program_id(0)
    t = pl.program_id(1)
    tm = x_ref.shape[0]

    @pl.when(t == 0)
    def _():
        for c in range(NCHUNK):
            ubuf_ref[c, 0:CONV_HIST_PAD, :] = jnp.zeros((CONV_HIST_PAD, LANES), F32)
            xrbuf_ref[c, 0:LRU_HIST_PAD, :] = jnp.zeros((LRU_HIST_PAD, LANES), F32)
        hcarry_ref[...] = jnp.zeros(hcarry_ref.shape, F32)
        _store_taps(cws_ref, cw_ref[...])
        _store_taps(lcws_ref, lcw_ref[...])

    xn = _rmsnorm(x_ref[...], gpre_ref[...]).astype(BF16)

    def in_proj(c0, c1):
        proj_ref[:, c0:c1] = jnp.dot(xn, win_ref[:, c0:c1], preferred_element_type=F32)

    def conv_lane_chunk(c):
        in_proj(2 * c * LANES, 2 * (c + 1) * LANES)
        glu = (proj_ref[:, 2 * c * LANES:(2 * c + 1) * LANES]
               * _sigmoid(proj_ref[:, (2 * c + 1) * LANES:(2 * c + 2) * LANES]))
        ubuf_ref[c, CONV_HIST_PAD:CONV_HIST_PAD + tm, :] = glu
        for i in range(tm // CONV_ROWS):
            r0 = i * CONV_ROWS
            conv_ref[r0:r0 + CONV_ROWS, c * LANES:(c + 1) * LANES] = _chunk_conv(
                ubuf_ref, cws_ref, c, r0 + (CONV_HIST_PAD - (CONV_K - 1)), CONV_K, CONV_ROWS // SUBLANES)

    in_proj(OFF_XR, OFF_GR)
    conv_lane_chunk(0)
    _store_slabs(xrbuf_ref, LRU_HIST_PAD, proj_ref[:, OFF_XR:OFF_XR + LRU_WIDTH])
    for i in range(tm // CONV_ROWS):
        r0 = i * CONV_ROWS
        xc = _window_conv(xrbuf_ref, lcws_ref, r0 + (LRU_HIST_PAD - (LRU_CONV_K - 1)), LRU_CONV_K,
                          CONV_ROWS // SUBLANES)
        xc_ref[r0:r0 + CONV_ROWS, :] = xc + lcb_ref[...]
    conv_lane_chunk(1)
    gate_ref[...] = jnp.dot(xc_ref[...].astype(BF16), wg_ref[...], preferred_element_type=F32)
    for c in range(2, NCHUNK):
        conv_lane_chunk(c)
    in_proj(OFF_GC, OFF_XR)
    in_proj(OFF_GR, IN_WIDTH)

    c_sp = LRU_C * _softplus(-lam_ref[...])
    ba = ba_ref[...]
    bx = bx_ref[...]

    def scan_chunk(i, hprev):
        r0 = i * SCAN_ROWS
        hs = []
        for j in range(SCAN_ROWS // SUBLANES):
            rows = slice(r0 + j * SUBLANES, r0 + (j + 1) * SUBLANES)
            a, bt = _lru_coeffs(gate_ref[rows, 0:LRU_WIDTH], gate_ref[rows, LRU_WIDTH:2 * LRU_WIDTH],
                                xc_ref[rows, :], ba, bx, c_sp)
            h = _group_scan(a, bt, hprev, SUBLANES)
            hprev = jnp.broadcast_to(h[SUBLANES - 1:SUBLANES, :], (SUBLANES, LRU_WIDTH))
            hs.append(h)
        h16 = jnp.concatenate(hs, axis=0)
        rr = h16 * _silu(proj_ref[r0:r0 + SCAN_ROWS, OFF_GR:OFF_GR + LRU_WIDTH])
        mix_ref[r0:r0 + SCAN_ROWS, CONV_WIDTH:CONV_WIDTH + LRU_WIDTH] = rr.astype(BF16)
        return hprev

    def norm_chunk(i):
        r0 = i * CONV_ROWS
        c = _layernorm(conv_ref[r0:r0 + CONV_ROWS, :] + cb_ref[...], lng_ref[...], lnb_ref[...])
        c = _silu(c) * _silu(proj_ref[r0:r0 + CONV_ROWS, OFF_GC:OFF_GC + CONV_WIDTH])
        mix_ref[r0:r0 + CONV_ROWS, 0:CONV_WIDTH] = c.astype(BF16)

    n_scan = tm // SCAN_ROWS
    n_conv = tm // CONV_ROWS
    hprev = hcarry_ref[...]
    for i in range(n_scan):
        hprev = scan_chunk(i, hprev)
        if (i + 1) % (n_scan // n_conv) == 0:
            norm_chunk((i + 1) // (n_scan // n_conv) - 1)
    hlast = hprev
    hcarry_ref[...] = hlast

    for c in range(NCHUNK):
        ubuf_ref[c, 0:CONV_HIST_PAD, :] = ubuf_ref[c, tm:tm + CONV_HIST_PAD, :]
        xrbuf_ref[c, 0:LRU_HIST_PAD, :] = xrbuf_ref[c, tm:tm + LRU_HIST_PAD, :]

    for h in range(MEM_HEADS):
        cols = slice(h * MEM_HEAD_DIM, (h + 1) * MEM_HEAD_DIM)
        q = proj_ref[:, OFF_Q + h * MEM_HEAD_DIM:OFF_Q + (h + 1) * MEM_HEAD_DIM].astype(BF16)
        o = _attend(q, k_ref[:, cols], v_ref[:, cols])
        o = o * _silu(proj_ref[:, OFF_GQ + h * MEM_HEAD_DIM:OFF_GQ + (h + 1) * MEM_HEAD_DIM])
        mix_ref[:, CONV_WIDTH + LRU_WIDTH + h * MEM_HEAD_DIM:
                CONV_WIDTH + LRU_WIDTH + (h + 1) * MEM_HEAD_DIM] = o.astype(BF16)

    def out_proj(c0, c1):
        return jnp.dot(mix_ref[:, c0:c1], wout_ref[c0:c1, :], preferred_element_type=F32)

    out = (out_proj(0, CONV_WIDTH) + out_proj(CONV_WIDTH + LRU_WIDTH, MIX_WIDTH)
           + out_proj(CONV_WIDTH, CONV_WIDTH + LRU_WIDTH))
    y_ref[...] = x_ref[...] + _rmsnorm(out, gpost_ref[...])

    @pl.when(t == pl.num_programs(1) - 1)
    def _():
        nconv_ref[...] = _load_slabs(ubuf_ref, CONV_HIST_PAD - (CONV_K - 1), CONV_K - 1)
        nlru_ref[...] = _load_slabs(xrbuf_ref, LRU_HIST_PAD - (LRU_CONV_K - 1), LRU_CONV_K - 1)
        nh_ref[pl.ds(b, 1), :] = hlast[0:1, :]


def _const_spec(shape, l):
    nd = len(shape)
    return pl.BlockSpec((None,) + tuple(shape[1:]), lambda *_: (l,) + (0,) * (nd - 1))


def _prompt_layer(l, x, mk, mv, p):
    batch, seq, _ = x.shape
    tm = PROMPT_TM
    grid = (batch, seq // tm)
    row_params = [p["norm_pre_g"], p["w_in"], p["conv_w"], p["conv_b"], p["conv_ln_g"], p["conv_ln_b"],
                  p["lru_conv_w"], p["lru_conv_b"], p["w_gate"], p["lru_ba"], p["lru_bx"], p["lru_lambda"],
                  p["w_out"], p["norm_post_g"]]
    in_specs = [
        pl.BlockSpec((None, tm, D_MODEL), lambda b, t: (b, t, 0)),
        pl.BlockSpec((None, None, N_MEM, MEM_WIDTH), lambda b, t: (l, b, 0, 0)),
        pl.BlockSpec((None, None, N_MEM, MEM_WIDTH), lambda b, t: (l, b, 0, 0)),
    ] + [_const_spec(a.shape, l) for a in row_params]
    out_shape = [
        jax.ShapeDtypeStruct((batch, seq, D_MODEL), F32),
        jax.ShapeDtypeStruct((batch, CONV_K - 1, CONV_WIDTH), F32),
        jax.ShapeDtypeStruct((batch, LRU_CONV_K - 1, LRU_WIDTH), F32),
        jax.ShapeDtypeStruct((batch, LRU_WIDTH), F32),
    ]
    out_specs = [
        pl.BlockSpec((None, tm, D_MODEL), lambda b, t: (b, t, 0)),
        pl.BlockSpec((None, CONV_K - 1, CONV_WIDTH), lambda b, t: (b, 0, 0)),
        pl.BlockSpec((None, LRU_CONV_K - 1, LRU_WIDTH), lambda b, t: (b, 0, 0)),
        pl.BlockSpec((batch, LRU_WIDTH), lambda b, t: (0, 0)),
    ]
    scratch = [
        pltpu.VMEM((tm, IN_WIDTH), F32),
        pltpu.VMEM((tm, 2 * LRU_WIDTH), F32),
        pltpu.VMEM((tm, CONV_WIDTH), F32),
        pltpu.VMEM((NCHUNK, CONV_HIST_PAD + tm, LANES), F32),
        pltpu.VMEM((NCHUNK, LRU_HIST_PAD + tm, LANES), F32),
        pltpu.VMEM((NCHUNK, CONV_K * SUBLANES, LANES), F32),
        pltpu.VMEM((NCHUNK, LRU_CONV_K * SUBLANES, LANES), F32),
        pltpu.VMEM((tm, LRU_WIDTH), F32),
        pltpu.VMEM((tm, MIX_WIDTH), BF16),
        pltpu.VMEM((SUBLANES, LRU_WIDTH), F32),
    ]
    return pl.pallas_call(
        _prompt_layer_kernel,
        grid=grid,
        in_specs=in_specs,
        out_specs=out_specs,
        out_shape=out_shape,
        scratch_shapes=scratch,
        compiler_params=pltpu.CompilerParams(dimension_semantics=("arbitrary", "arbitrary"),
                                             vmem_limit_bytes=VMEM_LIMIT),
        name=f"prompt_layer{l}",
    )(x, mk, mv, *row_params)


def _sample_in_kernel(x_ref, cconv_ref, clru_ref, h0_ref, gpre_ref, win_ref, cw_ref, cb_ref, lng_ref, lnb_ref,
                      lcw_ref, lcb_ref, wg_ref, ba_ref, bx_ref, lam_ref, nconv_all_ref,
                      mixcr_ref, qg_ref, nconv_ref, nlru_ref, nh_ref,
                      proj_ref, u_ref, xr_ref, cwin_ref, lwin_ref, cws_ref, lcws_ref,
                      conv_ref, xc_ref, h0rep_ref, h_ref):
    rows = x_ref.shape[0]
    nb = cconv_ref.shape[0]
    t_new = rows // nb
    per_group = SUBLANES // t_new
    hist_c = CONV_K - 1
    hist_l = LRU_CONV_K - 1
    row8 = lax.broadcasted_iota(jnp.int32, (SUBLANES, CONV_WIDTH), 0)

    @pl.when(pl.program_id(0) == 0)
    def _():
        _store_taps(cws_ref, cw_ref[...])
        _store_taps(lcws_ref, lcw_ref[...])

    xn = _rmsnorm(x_ref[...], gpre_ref[...]).astype(BF16)
    proj_ref[...] = jnp.dot(xn, win_ref[...], preferred_element_type=F32)
    for c in range(NCHUNK):
        u_ref[:, c * LANES:(c + 1) * LANES] = (proj_ref[:, 2 * c * LANES:(2 * c + 1) * LANES]
                                               * _sigmoid(proj_ref[:, (2 * c + 1) * LANES:(2 * c + 2) * LANES]))
    xr_ref[...] = proj_ref[:, OFF_XR:OFF_XR + LRU_WIDTH]
    qg_ref[...] = proj_ref[:, OFF_Q:OFF_Q + 2 * MEM_WIDTH]

    c_new0 = CONV_HIST_PAD
    l_new0 = LRU_HIST_PAD

    def per_group_body(g, carry):
        r0 = pl.multiple_of(g * SUBLANES, SUBLANES)
        _store_slabs(cwin_ref, c_new0, u_ref[pl.ds(r0, SUBLANES), :])
        _store_slabs(lwin_ref, l_new0, xr_ref[pl.ds(r0, SUBLANES), :])
        conv8 = jnp.zeros((SUBLANES, CONV_WIDTH), F32)
        xc8 = jnp.zeros((SUBLANES, LRU_WIDTH), F32)
        h08 = jnp.zeros((SUBLANES, LRU_WIDTH), F32)
        for j in range(per_group):
            bi = g * per_group + j
            mine = (row8 // t_new) == j
            c0 = c_new0 + j * t_new - hist_c
            _store_slabs(cwin_ref, c0, cconv_ref[bi])
            nconv_ref[bi] = _load_slabs(cwin_ref, c0 + t_new, hist_c)
            acc = _window_conv(cwin_ref, cws_ref, c0, CONV_K, 1)
            conv8 = jnp.where(mine, pltpu.roll(acc, j * t_new, 0) if j else acc, conv8)
            l0 = l_new0 + j * t_new - hist_l
            _store_slabs(lwin_ref, l0, clru_ref[bi])
            nlru_ref[bi] = _load_slabs(lwin_ref, l0 + t_new, hist_l)
            xc = _window_conv(lwin_ref, lcws_ref, l0, LRU_CONV_K, 1)
            xc8 = jnp.where(mine, pltpu.roll(xc, j * t_new, 0) if j else xc, xc8)
            h08 = jnp.where(mine, jnp.broadcast_to(h0_ref[bi], (SUBLANES, LRU_WIDTH)), h08)
        conv_ref[pl.ds(r0, SUBLANES), :] = conv8 + cb_ref[...]
        xc_ref[pl.ds(r0, SUBLANES), :] = xc8 + lcb_ref[...]
        h0rep_ref[pl.ds(r0, SUBLANES), :] = h08
        return carry

    lax.fori_loop(0, rows // SUBLANES, per_group_body, 0)

    c = _layernorm(conv_ref[...], lng_ref[...], lnb_ref[...])
    c = _silu(c) * _silu(proj_ref[:, OFF_GC:OFF_GC + CONV_WIDTH])
    mixcr_ref[:, 0:CONV_WIDTH] = c.astype(BF16)

    proj_ref[:, 0:2 * LRU_WIDTH] = jnp.dot(xc_ref[...].astype(BF16), wg_ref[...], preferred_element_type=F32)
    c_sp = LRU_C * _softplus(-lam_ref[...])
    ba = ba_ref[...]
    bx = bx_ref[...]

    def scan_group(g, carry):
        rs = pl.ds(pl.multiple_of(g * SUBLANES, SUBLANES), SUBLANES)
        a, bt = _lru_coeffs(proj_ref[rs, 0:LRU_WIDTH], proj_ref[rs, LRU_WIDTH:2 * LRU_WIDTH],
                            xc_ref[rs, :], ba, bx, c_sp)
        h = _group_scan(a, bt, h0rep_ref[rs, :], t_new)
        h_ref[rs, :] = h * _silu(proj_ref[rs, OFF_GR:OFF_GR + LRU_WIDTH])
        for j in range(per_group):
            nh_ref[g * per_group + j] = h[(j + 1) * t_new - 1:(j + 1) * t_new, :]
        return carry

    lax.fori_loop(0, rows // SUBLANES, scan_group, 0)
    mixcr_ref[:, CONV_WIDTH:CONV_WIDTH + LRU_WIDTH] = h_ref[...].astype(BF16)


def _sample_in(l, xs2d, cache_conv, cache_lru_conv, state_lru_h4, nconv_all, p):
    rows = xs2d.shape[0]
    depth, nbatch = cache_conv.shape[0], cache_conv.shape[1]
    t_new = rows // nbatch
    assert SUBLANES % t_new == 0
    rb = SAMPLE_RB
    rt = rb * t_new
    grid = (nbatch // rb,)
    row_params = [p["norm_pre_g"], p["w_in"], p["conv_w"], p["conv_b"], p["conv_ln_g"], p["conv_ln_b"],
                  p["lru_conv_w"], p["lru_conv_b"], p["w_gate"], p["lru_ba"], p["lru_bx"], p["lru_lambda"]]
    in_specs = [
        pl.BlockSpec((rt, D_MODEL), lambda i: (i, 0)),
        pl.BlockSpec((None, rb, CONV_K - 1, CONV_WIDTH), lambda i: (l, i, 0, 0)),
        pl.BlockSpec((None, rb, LRU_CONV_K - 1, LRU_WIDTH), lambda i: (l, i, 0, 0)),
        pl.BlockSpec((None, rb, 1, LRU_WIDTH), lambda i: (l, i, 0, 0)),
    ] + [_const_spec(a.shape, l) for a in row_params] + [pl.BlockSpec(memory_space=pl.ANY)]
    out_shape = [
        jax.ShapeDtypeStruct((rows, CONV_WIDTH + LRU_WIDTH), BF16),
        jax.ShapeDtypeStruct((rows, 2 * MEM_WIDTH), F32),
        jax.ShapeDtypeStruct((depth, nbatch, CONV_K - 1, CONV_WIDTH), F32),
        jax.ShapeDtypeStruct((nbatch, LRU_CONV_K - 1, LRU_WIDTH), F32),
        jax.ShapeDtypeStruct((nbatch, 1, LRU_WIDTH), F32),
    ]
    out_specs = [
        pl.BlockSpec((rt, CONV_WIDTH + LRU_WIDTH), lambda i: (i, 0)),
        pl.BlockSpec((rt, 2 * MEM_WIDTH), lambda i: (i, 0)),
        pl.BlockSpec((None, rb, CONV_K - 1, CONV_WIDTH), lambda i: (l, i, 0, 0)),
        pl.BlockSpec((rb, LRU_CONV_K - 1, LRU_WIDTH), lambda i: (i, 0, 0)),
        pl.BlockSpec((rb, 1, LRU_WIDTH), lambda i: (i, 0, 0)),
    ]
    scratch = [
        pltpu.VMEM((rt, IN_WIDTH), F32),
        pltpu.VMEM((rt, CONV_WIDTH), F32),
        pltpu.VMEM((rt, LRU_WIDTH), F32),
        pltpu.VMEM((NCHUNK, CONV_HIST_PAD + 2 * SUBLANES, LANES), F32),
        pltpu.VMEM((NCHUNK, LRU_HIST_PAD + 2 * SUBLANES, LANES), F32),
        pltpu.VMEM((NCHUNK, CONV_K * SUBLANES, LANES), F32),
        pltpu.VMEM((NCHUNK, LRU_CONV_K * SUBLANES, LANES), F32),
        pltpu.VMEM((rt, CONV_WIDTH), F32),
        pltpu.VMEM((rt, LRU_WIDTH), F32),
        pltpu.VMEM((rt, LRU_WIDTH), F32),
        pltpu.VMEM((rt, LRU_WIDTH), F32),
    ]
    return pl.pallas_call(
        _sample_in_kernel,
        grid=grid,
        in_specs=in_specs,
        out_specs=out_specs,
        out_shape=out_shape,
        scratch_shapes=scratch,
        input_output_aliases={4 + len(row_params): 2},
        compiler_params=pltpu.CompilerParams(dimension_semantics=("arbitrary",),
                                             vmem_limit_bytes=VMEM_LIMIT),
        name=f"sample_in{l}",
    )(xs2d, cache_conv, cache_lru_conv, state_lru_h4, *row_params, nconv_all)


def _sample_out_kernel(x_ref, mixcr_ref, qg_ref, k_ref, v_ref, wout_ref, gpost_ref, y_ref, mix_ref):
    i = pl.program_id(0)
    nb = k_ref.shape[0]
    rt = qg_ref.shape[0]
    t_new = rt // nb
    per_group = SUBLANES // t_new
    r0 = pl.multiple_of(i * rt, rt)

    nq = MEM_HEADS * SUBLANES
    nkv = N_MEM * MEM_HEADS
    q_head = lax.broadcasted_iota(jnp.int32, (nq, nkv), 0) // SUBLANES
    kv_head = lax.broadcasted_iota(jnp.int32, (nq, nkv), 1) % MEM_HEADS
    valid = q_head == kv_head
    row_batch = lax.broadcasted_iota(jnp.int32, (SUBLANES, MEM_WIDTH), 0) // t_new
    scale = 1.0 / math.sqrt(MEM_HEAD_DIM)

    outs = []
    for g in range(rt // SUBLANES):
        rs = slice(g * SUBLANES, (g + 1) * SUBLANES)
        q8 = qg_ref[rs, 0:MEM_WIDTH]
        q2 = jnp.concatenate([q8[:, h * MEM_HEAD_DIM:(h + 1) * MEM_HEAD_DIM] for h in range(MEM_HEADS)],
                             axis=0).astype(BF16)
        o8 = jnp.zeros((SUBLANES, MEM_WIDTH), F32)
        for j in range(per_group):
            bi = g * per_group + j
            s = lax.dot_general(q2, k_ref[bi].astype(BF16), (((1,), (1,)), ((), ())),
                                preferred_element_type=F32)
            s = jnp.where(valid, s * scale, -1e30)
            e = jnp.exp(s - jnp.max(s, axis=-1, keepdims=True))
            l = jnp.sum(e, axis=-1, keepdims=True)
            o2 = jnp.dot(e.astype(BF16), v_ref[bi].astype(BF16), preferred_element_type=F32) / l
            o_b = jnp.concatenate([o2[h * SUBLANES:(h + 1) * SUBLANES, :] for h in range(MEM_HEADS)], axis=1)
            o8 = jnp.where(row_batch == j, o_b, o8)
        outs.append(o8 * _silu(qg_ref[rs, MEM_WIDTH:2 * MEM_WIDTH]))
    o = jnp.concatenate(outs, axis=0)
    mix_ref[pl.ds(r0, rt), 0:CONV_WIDTH + LRU_WIDTH] = mixcr_ref[...]
    mix_ref[pl.ds(r0, rt), CONV_WIDTH + LRU_WIDTH:MIX_WIDTH] = o.astype(BF16)

    @pl.when(i == pl.num_programs(0) - 1)
    def _():
        out = jnp.dot(mix_ref[...], wout_ref[...], preferred_element_type=F32)
        y_ref[...] = x_ref[...] + _rmsnorm(out, gpost_ref[...])


def _sample_out(l, xs2d, mixcr, qg, cache_k, cache_v, p):
    rows = xs2d.shape[0]
    nbatch = cache_k.shape[1]
    t_new = rows // nbatch
    ab = SAMPLE_AB
    rt = ab * t_new
    grid = (nbatch // ab,)
    in_specs = [
        pl.BlockSpec((rows, D_MODEL), lambda i: (0, 0)),
        pl.BlockSpec((rt, CONV_WIDTH + LRU_WIDTH), lambda i: (i, 0)),
        pl.BlockSpec((rt, 2 * MEM_WIDTH), lambda i: (i, 0)),
        pl.BlockSpec((None, ab, N_MEM * MEM_HEADS, MEM_HEAD_DIM), lambda i: (l, i, 0, 0)),
        pl.BlockSpec((None, ab, N_MEM * MEM_HEADS, MEM_HEAD_DIM), lambda i: (l, i, 0, 0)),
        _const_spec(p["w_out"].shape, l),
        _const_spec(p["norm_post_g"].shape, l),
    ]
    return pl.pallas_call(
        _sample_out_kernel,
        grid=grid,
        in_specs=in_specs,
        out_specs=pl.BlockSpec((rows, D_MODEL), lambda i: (0, 0)),
        out_shape=jax.ShapeDtypeStruct((rows, D_MODEL), F32),
        scratch_shapes=[pltpu.VMEM((rows, MIX_WIDTH), BF16)],
        compiler_params=pltpu.CompilerParams(dimension_semantics=("arbitrary",),
                                             vmem_limit_bytes=VMEM_LIMIT),
        name=f"sample_out{l}",
    )(xs2d, mixcr, qg, cache_k, cache_v, p["w_out"], p["norm_post_g"])


def _block_diag(w):
    h, d, _ = w.shape
    eye = jnp.eye(h, dtype=w.dtype)
    return (w[:, :, None, :] * eye[:, None, :, None]).reshape(h * d, h * d)


def kernel(x_prompt, x_sample, mem_prompt, cache_conv, cache_lru_conv, state_lru_h, cache_mem_k, cache_mem_v,
           norm_pre_g, w_in, conv_w, conv_b, conv_ln_g, conv_ln_b, lru_conv_w, lru_conv_b, lru_wa, lru_ba,
           lru_wx, lru_bx, lru_lambda, mem_norm_g, w_mem_k, w_mem_v, w_out, norm_post_g):
    depth = w_in.shape[0]
    dec_batch, dec_seq, _ = x_sample.shape

    def vec(a):
        return a.reshape(depth, 1, a.shape[-1])

    w_gate = jnp.concatenate([jax.vmap(_block_diag)(lru_wa), jax.vmap(_block_diag)(lru_wx)], axis=-1)
    glu_cols = []
    for c in range(NCHUNK):
        glu_cols.append(w_in[:, :, OFF_A + c * LANES:OFF_A + (c + 1) * LANES])
        glu_cols.append(w_in[:, :, OFF_B + c * LANES:OFF_B + (c + 1) * LANES])
    w_in_glu = jnp.concatenate(glu_cols + [w_in[:, :, OFF_GC:]], axis=-1)
    params = {
        "norm_pre_g": vec(norm_pre_g), "w_in": w_in_glu.astype(BF16),
        "conv_w": conv_w, "conv_b": vec(conv_b), "conv_ln_g": vec(conv_ln_g), "conv_ln_b": vec(conv_ln_b),
        "lru_conv_w": lru_conv_w, "lru_conv_b": vec(lru_conv_b), "w_gate": w_gate.astype(BF16),
        "lru_ba": vec(lru_ba), "lru_bx": vec(lru_bx), "lru_lambda": vec(lru_lambda),
        "w_out": w_out.astype(BF16), "norm_post_g": vec(norm_post_g),
    }

    p_mk, p_mv, p_mk_bf, p_mv_bf = _mem_kv(mem_prompt, mem_norm_g, w_mem_k.astype(BF16), w_mem_v.astype(BF16))
    s_mk = cache_mem_k.reshape(depth, dec_batch, N_MEM * MEM_HEADS, MEM_HEAD_DIM)
    s_mv = cache_mem_v.reshape(depth, dec_batch, N_MEM * MEM_HEADS, MEM_HEAD_DIM)
    s_h0 = state_lru_h.reshape(depth, dec_batch, 1, LRU_WIDTH)

    xp = x_prompt
    xs = x_sample.reshape(dec_batch * dec_seq, D_MODEL)
    p_conv, p_lconv, p_h, s_lconv, s_h = [], [], [], [], []
    s_conv = jnp.zeros(cache_conv.shape, cache_conv.dtype)
    for l in range(depth):
        xp, cb, lb, hh = _prompt_layer(l, xp, p_mk_bf, p_mv_bf, params)
        p_conv.append(cb); p_lconv.append(lb); p_h.append(hh)
        mixcr, qg, s_conv, lb2, hh2 = _sample_in(l, xs, cache_conv, cache_lru_conv, s_h0, s_conv, params)
        xs = _sample_out(l, xs, mixcr, qg, s_mk, s_mv, params)
        s_lconv.append(lb2); s_h.append(hh2.reshape(dec_batch, LRU_WIDTH))

    mem_shape = (depth, x_prompt.shape[0], N_MEM, MEM_HEADS, MEM_HEAD_DIM)
    return (xp, xs.reshape(dec_batch, dec_seq, D_MODEL),
            jnp.stack(p_conv), jnp.stack(p_lconv), jnp.stack(p_h),
            p_mk.reshape(mem_shape), p_mv.reshape(mem_shape),
            s_conv, jnp.stack(s_lconv), jnp.stack(s_h))
```
